```python
import jax, jax.numpy as jnp
from jax import lax
import numpy as np

D_MODEL = 1024
BATCH = 16
SEQ = 4096
DEPTH = 2

CHUNK = 64
EPS = 1e-6
HA_HEADS = 4
HA_DK = 128
HA_DV = 128
HA_K = HA_HEADS * HA_DK
HA_V = HA_HEADS * HA_DV
RB_HEADS = 4
RB_DK = 128
RB_DV = 256
RB_K = RB_HEADS * RB_DK
RB_V = RB_HEADS * RB_DV
ROPE_BASE = 10000.0
IN_COLS = (HA_K, HA_K, HA_V, HA_V, RB_K, RB_K, RB_V, RB_V, D_MODEL, D_MODEL)
IN_TOTAL = sum(IN_COLS)
N_EXPERTS = 16
N_GROUPS = 4
EXP_PER_GROUP = N_EXPERTS // N_GROUPS
TOP_K = 2
D_EXPERT = 512

kernel_name = 'hybrid_hgrn2_retention_groupmoe_adaln'


def rms_norm(x, g):
    xf = x.astype(jnp.float32)
    y = xf * lax.rsqrt(jnp.mean(xf * xf, axis=-1, keepdims=True) + EPS)
    return (y * g.astype(jnp.float32)).astype(x.dtype)


def head_layer_norm(o, g):
    mu = jnp.mean(o, axis=-1, keepdims=True)
    oc = o - mu
    var = jnp.mean(oc * oc, axis=-1, keepdims=True)
    return oc * lax.rsqrt(var + EPS) * g.astype(jnp.float32)


def to_chunks(t, heads):
    b, s, w = t.shape
    return t.reshape(b, s // CHUNK, CHUNK, heads, w // heads).transpose(1, 0, 3, 2, 4)


def from_chunks(t):
    n, b, h, c, d = t.shape
    return t.transpose(1, 0, 3, 2, 4).reshape(b, n * c, h, d)


def rotary(t, heads):
    b, s, w = t.shape
    d = w // heads
    half = d // 2
    t = t.reshape(b, s, heads, d)
    inv = ROPE_BASE ** (-jnp.arange(half, dtype=jnp.float32) / half)
    ang = jnp.arange(s, dtype=jnp.float32)[:, None] * inv[None, :]
    cos = jnp.cos(ang)[None, :, None, :]
    sin = jnp.sin(ang)[None, :, None, :]
    t1, t2 = t[..., :half], t[..., half:]
    out = jnp.concatenate([t1 * cos - t2 * sin, t1 * sin + t2 * cos], axis=-1)
    return out.reshape(b, s, w)


def hgrn2_scan(q, k, v, log_f):
    n, b, h, c, dk = q.shape
    dv = v.shape[-1]
    causal = jnp.tril(jnp.ones((c, c), dtype=bool))[:, :, None]

    def step(state, xs):
        qn, kn, vn, gn = xs
        cum = jnp.cumsum(gn, axis=2)
        diff = cum[:, :, :, None, :] - cum[:, :, None, :, :]
        decay = jnp.exp(jnp.where(causal, diff, -jnp.inf))
        scores = jnp.einsum('bhjd,bhld,bhjld->bhjl', qn, kn, decay)
        out = (jnp.einsum('bhjl,bhlv->bhjv', scores, vn)
               + jnp.einsum('bhjd,bhdv->bhjv', qn * jnp.exp(cum), state))
        last = cum[:, :, -1:, :]
        state = (jnp.exp(last[:, :, 0, :, None]) * state
                 + jnp.einsum('bhld,bhlv->bhdv', kn * jnp.exp(last - cum), vn))
        return state, out

    init = jnp.zeros((b, h, dk, dv), jnp.float32)
    _, out = lax.scan(step, init, (q, k, v, log_f))
    return out


def retention_scan(q, k, v):
    n, b, h, c, dk = q.shape
    dv = v.shape[-1]
    log_gamma = jnp.log(1.0 - 2.0 ** (-5.0 - jnp.arange(h, dtype=jnp.float32)))
    pos = jnp.arange(c, dtype=jnp.float32)
    intra = jnp.exp(log_gamma[:, None, None] * jnp.abs(pos[:, None] - pos[None, :]))
    q_dec = jnp.exp(log_gamma[:, None] * (pos + 1.0))[:, :, None]
    k_dec = jnp.exp(log_gamma[:, None] * (c - 1.0 - pos))[:, :, None]
    chunk_dec = jnp.exp(log_gamma * c)[:, None, None]

    def step(state, xs):
        qn, kn, vn = xs
        scores = jnp.einsum('bhjd,bhld->bhjl', qn, kn) * intra
        out = (jnp.einsum('bhjl,bhlv->bhjv', scores, vn)
               + jnp.einsum('bhjd,bhdv->bhjv', qn, state) * q_dec)
        state = chunk_dec * state + jnp.einsum('bhld,bhlv->bhdv', kn * k_dec, vn)
        return state, out

    init = jnp.zeros((b, h, dk, dv), jnp.float32)
    _, out = lax.scan(step, init, (q, k, v))
    return out


def hybrid_mixer(h, w_in, lb, g_hgrn, g_ret, w_branch_a, w_branch_b, w_out):
    bsz, s, _ = h.shape
    f32 = jnp.float32
    z = h @ w_in
    bounds = np.cumsum(IN_COLS)[:-1].tolist()
    q_a, f_a, i_a, og_a, q_b, k_b, v_b, og_b, m_a, m_b = jnp.split(z, bounds, axis=-1)

    f = lb + (1.0 - lb) * jax.nn.sigmoid(f_a.astype(f32))
    o_a = hgrn2_scan(to_chunks(jax.nn.silu(q_a.astype(f32)), HA_HEADS),
                     to_chunks(1.0 - f, HA_HEADS),
                     to_chunks(i_a.astype(f32), HA_HEADS),
                     to_chunks(jnp.log(f), HA_HEADS))
    o_a = rms_norm(from_chunks(o_a), g_hgrn).reshape(bsz, s, HA_V)
    y_a = (o_a * jax.nn.silu(og_a.astype(f32))).astype(h.dtype) @ w_branch_a

    qr = rotary(q_b.astype(f32), RB_HEADS) * (RB_DK ** -0.5)
    kr = rotary(k_b.astype(f32), RB_HEADS)
    o_b = retention_scan(to_chunks(qr, RB_HEADS), to_chunks(kr, RB_HEADS),
                         to_chunks(v_b.astype(f32), RB_HEADS))
    o_b = head_layer_norm(from_chunks(o_b), g_ret).reshape(bsz, s, RB_V)
    y_b = (o_b * jax.nn.silu(og_b.astype(f32))).astype(h.dtype) @ w_branch_b

    y = jax.nn.sigmoid(m_a) * y_a + jax.nn.sigmoid(m_b) * y_b
    return y @ w_out


def grouped_moe(h, w_router, b_router, w_gate, w_up, w_down):
    bsz, s, d = h.shape
    t = h.reshape(-1, d)
    n_tok = t.shape[0]
    scores = jax.nn.sigmoid((t @ w_router).astype(jnp.float32))
    biased = scores + b_router.astype(jnp.float32)
    grp = biased.reshape(n_tok, N_GROUPS, EXP_PER_GROUP)
    grp_score = lax.top_k(grp, TOP_K)[0].sum(axis=-1)
    g_sel = jnp.argmax(grp_score, axis=-1)
    in_grp = grp[jnp.arange(n_tok), g_sel]
    _, local = lax.top_k(in_grp, TOP_K)
    expert_idx = g_sel[:, None] * EXP_PER_GROUP + local
    w_sel = jnp.take_along_axis(scores, expert_idx, axis=1)
    w_sel = w_sel / jnp.sum(w_sel, axis=-1, keepdims=True)
    combine = jnp.sum(jax.nn.one_hot(expert_idx, N_EXPERTS, dtype=jnp.float32) * w_sel[..., None], axis=1)
    out = jnp.zeros((n_tok, d), jnp.float32)
    for e in range(N_EXPERTS):
        he = jax.nn.silu(t @ w_gate[e]) * (t @ w_up[e])
        out = out + combine[:, e:e + 1] * (he @ w_down[e]).astype(jnp.float32)
    return out.astype(h.dtype).reshape(bsz, s, d)


def setup_inputs(seed: int = 0) -> dict:
    key = jax.random.key(seed)
    ks = jax.random.split(key, 20)
    f32 = jnp.float32
    D = D_MODEL

    def nrm(k, shape, scale):
        return jax.random.normal(k, shape, f32) * scale

    return {
        'x': nrm(ks[0], (BATCH, SEQ, D), 1.0),
        'c': nrm(ks[1], (BATCH, D), 1.0),
        'w_ada': nrm(ks[2], (DEPTH, D, 6 * D), 0.5 * D ** -0.5),
        'b_ada': nrm(ks[3], (DEPTH, 6 * D), 0.02),
        'g_norm1': 1.0 + nrm(ks[4], (DEPTH, D), 0.02),
        'g_norm2': 1.0 + nrm(ks[5], (DEPTH, D), 0.02),
        'w_in': nrm(ks[6], (DEPTH, D, IN_TOTAL), D ** -0.5),
        'lb_logits': nrm(ks[7], (DEPTH, HA_K), 0.1),
        'g_hgrn': 1.0 + nrm(ks[8], (DEPTH, HA_HEADS, HA_DV), 0.02),
        'g_ret': 1.0 + nrm(ks[9], (DEPTH, RB_HEADS, RB_DV), 0.02),
        'w_branch_a': nrm(ks[10], (DEPTH, HA_V, D), HA_V ** -0.5),
        'w_branch_b': nrm(ks[11], (DEPTH, RB_V, D), RB_V ** -0.5),
        'w_out': nrm(ks[12], (DEPTH, D, D), D ** -0.5),
        'w_router': nrm(ks[13], (D, N_EXPERTS), D ** -0.5),
        'b_router': nrm(ks[14], (N_EXPERTS,), 0.01),
        'w_exp_gate': nrm(ks[15], (DEPTH, N_EXPERTS, D, D_EXPERT), D ** -0.5),
        'w_exp_up': nrm(ks[16], (DEPTH, N_EXPERTS, D, D_EXPERT), D ** -0.5),
        'w_exp_down': nrm(ks[17], (DEPTH, N_EXPERTS, D_EXPERT, D), D_EXPERT ** -0.5),
        'g_final': 1.0 + nrm(ks[18], (D,), 0.02),
    }


def reference(x, c, w_ada, b_ada, g_norm1, g_norm2, w_in, lb_logits, g_hgrn, g_ret,
              w_branch_a, w_branch_b, w_out, w_router, b_router, w_exp_gate, w_exp_up,
              w_exp_down, g_final):
    p = jax.nn.softmax(lb_logits.astype(jnp.float32), axis=0)
    lower_bounds = jnp.cumsum(p, axis=0) - p[0:1]
    c_act = jax.nn.silu(c)
    for l in range(DEPTH):
        mod = c_act @ w_ada[l] + b_ada[l]
        sh1, sc1, gt1, sh2, sc2, gt2 = jnp.split(mod[:, None, :], 6, axis=-1)
        h = rms_norm(x, g_norm1[l]) * (1.0 + sc1) + sh1
        x = x + gt1 * hybrid_mixer(h, w_in[l], lower_bounds[l], g_hgrn[l], g_ret[l],
                                   w_branch_a[l], w_branch_b[l], w_out[l])
        h = rms_norm(x, g_norm2[l]) * (1.0 + sc2) + sh2
        x = x + gt2 * grouped_moe(h, w_router, b_router, w_exp_gate[l], w_exp_up[l], w_exp_down[l])
    return rms_norm(x, g_final)
```

```python
import functools
import math

import jax
import jax.numpy as jnp
from jax import lax
from jax.experimental import pallas as pl
from jax.experimental.pallas import tpu as pltpu

F32 = jnp.float32
BF16 = jnp.bfloat16

EPS = 1e-6
CHUNK = 64
ROPE_BASE = 10000.0
HA_HEADS, HA_DK, HA_DV = 4, 128, 128
RB_HEADS, RB_DK, RB_DV = 4, 128, 256
HA_K, HA_V = HA_HEADS * HA_DK, HA_HEADS * HA_DV
RB_K, RB_V = RB_HEADS * RB_DK, RB_HEADS * RB_DV
N_EXPERTS, N_GROUPS, TOP_K = 16, 4, 2
EXP_PER_GROUP = N_EXPERTS // N_GROUPS

LANES = 128
SUBLANES = 8
VMEM_LIMIT_BYTES = 56 * 1024 * 1024

MIX_TILE = 256
MOE_TILE = 1024
MOD_COLS = 1536

_NT = (((1,), (1,)), ((), ()))
_TN = (((0,), (0,)), ((), ()))


def _sigmoid(v):
    return jax.nn.sigmoid(v)


def _silu(v):
    return v * jax.nn.sigmoid(v)


def _bdot(a, b, dims=None):
    a = a.astype(BF16)
    b = b.astype(BF16)
    if dims is None:
        return jnp.dot(a, b, preferred_element_type=F32)
    return lax.dot_general(a, b, dims, preferred_element_type=F32)


def _mod_kernel(c_ref, w_ref, b_ref, o_ref):
    o_ref[0] = _bdot(_silu(c_ref[...]), w_ref[0]) + b_ref[0]


def _modulation(c, w_ada, b_ada):
    depth, d, six_d = w_ada.shape
    bsz = c.shape[0]
    return pl.pallas_call(
        _mod_kernel,
        grid=(depth, six_d // MOD_COLS),
        in_specs=[
            pl.BlockSpec((bsz, d), lambda l, j: (0, 0)),
            pl.BlockSpec((1, d, MOD_COLS), lambda l, j: (l, 0, j)),
            pl.BlockSpec((1, 1, MOD_COLS), lambda l, j: (l, 0, j)),
        ],
        out_specs=pl.BlockSpec((1, bsz, MOD_COLS), lambda l, j: (l, 0, j)),
        out_shape=jax.ShapeDtypeStruct((depth, bsz, six_d), F32),
        compiler_params=pltpu.CompilerParams(
            dimension_semantics=("arbitrary", "arbitrary"),
            vmem_limit_bytes=VMEM_LIMIT_BYTES),
        name="adaln_modulation",
    )(c, w_ada, b_ada.reshape(depth, 1, six_d))


def _level_ref(cum_scr, cumh, cols, block):
    t = cumh.shape[0]
    half = block // 2
    if block >= SUBLANES:
        pieces = []
        for m in range(t // block):
            r = m * block + half - 1
            pieces.append(jnp.broadcast_to(cum_scr[r:r + 1, cols], (block, cumh.shape[1])))
        return pieces[0] if len(pieces) == 1 else jnp.concatenate(pieces, axis=0)
    pos = lax.broadcasted_iota(jnp.int32, cumh.shape, 0) & (block - 1)
    ref = cumh
    for off in range(-half, half):
        if off == 0:
            continue
        shifted = pltpu.roll(cumh, (-off) % t, 0)
        ref = jnp.where(pos == half - 1 - off, shifted, ref)
    return ref


def _mixer_kernel(x_ref, mod_ref, g1_ref, win_ref, lbl_ref, ghg_ref, gret_ref,
                  wa_ref, wb_ref, wo_ref, cos_ref, sin_ref, o_ref,
                  z_scr, sa_scr, sb_scr, cum_scr, k_scr, retw_scr, lvl_scr, tri_scr,
                  oa_scr, ob_scr, *, layer, depth):
    t = x_ref.shape[1]
    n_levels = t.bit_length() - 1
    step = pl.program_id(1)

    @pl.when(jnp.logical_and(pl.program_id(0) == 0, step == 0))
    def _build_constants():
        row = lax.broadcasted_iota(jnp.int32, (t, t), 0)
        col = lax.broadcasted_iota(jnp.int32, (t, t), 1)
        diff = row ^ col
        lvl = jnp.zeros((t, t), jnp.int32)
        for bit in range(n_levels):
            lvl = lvl + (diff >= (1 << bit)).astype(jnp.int32)
        lvl_scr[...] = jnp.where(col > row, -1, lvl)
        tri_scr[...] = (col <= row).astype(BF16)
        dist = jnp.abs(row - col).astype(F32)
        visible = (col // CHUNK) <= (row // CHUNK)
        for h in range(RB_HEADS):
            log_gamma = math.log(1.0 - 2.0 ** (-5.0 - h))
            retw_scr[h] = jnp.where(visible, jnp.exp(log_gamma * dist), 0.0)

    @pl.when(step == 0)
    def _reset_state():
        sa_scr[...] = jnp.zeros_like(sa_scr)
        sb_scr[...] = jnp.zeros_like(sb_scr)

    x = x_ref[0]
    modv = mod_ref[0, 0]
    sh1, sc1, gt1 = modv[0:1], modv[1:2], modv[2:3]
    ms = jnp.mean(x * x, axis=-1, keepdims=True)
    hn = (x * lax.rsqrt(ms + EPS)) * g1_ref[...]
    hb = (hn * (1.0 + sc1) + sh1).astype(BF16)
    n_in = win_ref.shape[1]
    seg = 1024
    for s in range(n_in // seg):
        z_scr[:, s * seg:(s + 1) * seg] = jnp.dot(
            hb, win_ref[:, s * seg:(s + 1) * seg], preferred_element_type=F32)

    rows = [lbl_ref[i:i + 1, :] for i in range(depth)]
    mx = functools.reduce(jnp.maximum, rows)
    ex = [jnp.exp(r - mx) for r in rows]
    tot = functools.reduce(lambda a, b: a + b, ex)
    lb = functools.reduce(lambda a, b: a + b, [e / tot for e in ex[:layer + 1]]) - ex[0] / tot

    o_q, o_f, o_i, o_og = 0, HA_K, 2 * HA_K, 2 * HA_K + HA_V
    o_qb = o_og + HA_V
    o_kb = o_qb + RB_K
    o_vb = o_kb + RB_K
    o_ogb = o_vb + RB_V
    o_ma = o_ogb + RB_V
    o_mb = o_ma + x.shape[1]

    f = lb + (1.0 - lb) * _sigmoid(z_scr[:, o_f:o_f + HA_K])
    k_scr[...] = 1.0 - f
    g = jnp.log(f)
    g_hi = g.astype(BF16)
    g_lo = (g - g_hi.astype(F32)).astype(BF16)
    tri = tri_scr[...]
    cum_scr[...] = (jnp.dot(tri, g_hi, preferred_element_type=F32)
                    + jnp.dot(tri, g_lo, preferred_element_type=F32))

    lvl = lvl_scr[...]
    for h in range(HA_HEADS):
        cols = slice(h * HA_DK, (h + 1) * HA_DK)
        q = _silu(z_scr[:, o_q + h * HA_DK:o_q + (h + 1) * HA_DK])
        k = k_scr[:, cols]
        v = z_scr[:, o_i + h * HA_DV:o_i + (h + 1) * HA_DV].astype(BF16)
        cumh = cum_scr[:, cols]
        scores = jnp.zeros((t, t), F32)
        for level in range(n_levels + 1):
            if level == 0:
                qs, ks = q, k
            else:
                ref = _level_ref(cum_scr, cumh, cols, 1 << level)
                e = jnp.exp(-jnp.abs(cumh - ref))
                qs, ks = q * e, k * e
            scores = jnp.where(lvl == level, _bdot(qs, ks, _NT), scores)
        st = sa_scr[h]
        last = cum_scr[t - 1:t, cols]
        o = _bdot(scores, v) + _bdot(q * jnp.exp(cumh), st, _NT)
        sa_scr[h] = st * jnp.exp(last) + _bdot(v, k * jnp.exp(last - cumh), _TN)
        on = o * lax.rsqrt(jnp.mean(o * o, axis=-1, keepdims=True) + EPS) * ghg_ref[:, cols]
        og = z_scr[:, o_og + h * HA_DV:o_og + (h + 1) * HA_DV]
        oa_scr[:, cols] = (on * _silu(og)).astype(BF16)

    cosf = cos_ref[...]
    sins = sin_ref[...]
    pos = lax.broadcasted_iota(jnp.int32, (t, RB_DK), 0).astype(F32)
    for h in range(RB_HEADS):
        log_gamma = math.log(1.0 - 2.0 ** (-5.0 - h))
        qb = z_scr[:, o_qb + h * RB_DK:o_qb + (h + 1) * RB_DK]
        kb = z_scr[:, o_kb + h * RB_DK:o_kb + (h + 1) * RB_DK]
        qr = (qb * cosf + pltpu.roll(qb, RB_DK // 2, 1) * sins) * (RB_DK ** -0.5)
        kr = kb * cosf + pltpu.roll(kb, RB_DK // 2, 1) * sins
        vcols = slice(h * RB_DV, (h + 1) * RB_DV)
        vb = z_scr[:, o_vb + h * RB_DV:o_vb + (h + 1) * RB_DV].astype(BF16)
        s = _bdot(qr, kr, _NT) * retw_scr[h]
        sb = sb_scr[h]
        q_dec = jnp.exp(log_gamma * (pos + 1.0))
        k_dec = jnp.exp(log_gamma * (float(t - 1) - pos))
        o = _bdot(s, vb) + _bdot(qr * q_dec, sb)
        sb_scr[h] = math.exp(log_gamma * t) * sb + _bdot(kr * k_dec, vb, _TN)
        mu = jnp.mean(o, axis=-1, keepdims=True)
        oc = o - mu
        var = jnp.mean(oc * oc, axis=-1, keepdims=True)
        on = oc * lax.rsqrt(var + EPS) * gret_ref[:, vcols]
        og = z_scr[:, o_ogb + h * RB_DV:o_ogb + (h + 1) * RB_DV]
        ob_scr[:, vcols] = (on * _silu(og)).astype(BF16)

    d = x.shape[1]
    ya = jnp.dot(oa_scr[...], wa_ref[...], preferred_element_type=F32)
    yb = jnp.dot(ob_scr[...], wb_ref[...], preferred_element_type=F32)
    y = _sigmoid(z_scr[:, o_ma:o_ma + d]) * ya + _sigmoid(z_scr[:, o_mb:o_mb + d]) * yb
    o_ref[0] = x + gt1 * _bdot(y, wo_ref[...])


def _const_spec(shape):
    return pl.BlockSpec(shape, lambda b, s: (0,) * len(shape), pipeline_mode=pl.Buffered(1))


def _mixer(x, mod, g1, w_in, lb_logits, g_hgrn, g_ret, w_a, w_b, w_o, cos_t, sin_t, *, layer):
    bsz, seq, d = x.shape
    depth = mod.shape[0]
    t = MIX_TILE
    n_in = w_in.shape[1]
    kern = functools.partial(_mixer_kernel, layer=layer, depth=depth)
    return pl.pallas_call(
        kern,
        grid=(bsz, seq // t),
        in_specs=[
            pl.BlockSpec((1, t, d), lambda b, s: (b, s, 0)),
            pl.BlockSpec((1, 1, 6, d), lambda b, s: (layer, b, 0, 0)),
            _const_spec((1, d)),
            _const_spec((d, n_in)),
            _const_spec((depth, HA_K)),
            _const_spec((1, HA_V)),
            _const_spec((1, RB_V)),
            _const_spec((HA_V, d)),
            _const_spec((RB_V, d)),
            _const_spec((d, d)),
            pl.BlockSpec((t, RB_DK), lambda b, s: (s, 0)),
            pl.BlockSpec((t, RB_DK), lambda b, s: (s, 0)),
        ],
        out_specs=pl.BlockSpec((1, t, d), lambda b, s: (b, s, 0)),
        out_shape=jax.ShapeDtypeStruct((bsz, seq, d), F32),
        scratch_shapes=[
            pltpu.VMEM((t, n_in), F32),
            pltpu.VMEM((HA_HEADS, HA_DV, HA_DK), F32),
            pltpu.VMEM((RB_HEADS, RB_DK, RB_DV), F32),
            pltpu.VMEM((t, HA_K), F32),
            pltpu.VMEM((t, HA_K), F32),
            pltpu.VMEM((RB_HEADS, t, t), F32),
            pltpu.VMEM((t, t), jnp.int32),
            pltpu.VMEM((t, t), BF16),
            pltpu.VMEM((t, HA_V), BF16),
            pltpu.VMEM((t, RB_V), BF16),
        ],
        compiler_params=pltpu.CompilerParams(
            dimension_semantics=("arbitrary", "arbitrary"),
            vmem_limit_bytes=VMEM_LIMIT_BYTES),
        name=f"mixer_layer{layer}",
    )(x, mod, g1, w_in, lb_logits, g_hgrn, g_ret, w_a, w_b, w_o, cos_t, sin_t)


def _lane_shift(a, delta):
    return pltpu.roll(a, (-delta) % a.shape[1], 1)


def _route(scores, biased):
    lane = lax.broadcasted_iota(jnp.int32, scores.shape, 1)
    pos = lane & (EXP_PER_GROUP - 1)
    grp = lane // EXP_PER_GROUP
    rank = jnp.zeros(scores.shape, jnp.int32)
    for delta in range(-(EXP_PER_GROUP - 1), EXP_PER_GROUP):
        if delta == 0:
            continue
        other = _lane_shift(biased, delta)
        ahead = (other > biased) | ((other == biased) & (delta < 0))
        in_grp = (pos + delta >= 0) & (pos + delta < EXP_PER_GROUP)
        rank = rank + (ahead & in_grp).astype(jnp.int32)
    top = rank < TOP_K
    kept = jnp.where(top, biased, 0.0)
    grp_score = kept
    for delta in range(-(EXP_PER_GROUP - 1), EXP_PER_GROUP):
        if delta == 0:
            continue
        in_grp = (pos + delta >= 0) & (pos + delta < EXP_PER_GROUP)
        grp_score = grp_score + jnp.where(in_grp, _lane_shift(kept, delta), 0.0)
    grank = jnp.zeros(scores.shape, jnp.int32)
    for dg in range(-(N_GROUPS - 1), N_GROUPS):
        if dg == 0:
            continue
        other = _lane_shift(grp_score, dg * EXP_PER_GROUP)
        ahead = (other > grp_score) | ((other == grp_score) & (dg < 0))
        in_rng = (grp + dg >= 0) & (grp + dg < N_GROUPS)
        grank = grank + (ahead & in_rng).astype(jnp.int32)
    sel = top & (grank == 0) & (lane < N_EXPERTS)
    w = jnp.where(sel, scores, 0.0)
    return w / jnp.sum(w, axis=-1, keepdims=True)


def _moe_kernel(x_ref, mod_ref, g2_ref, wr_ref, br_ref, wg_ref, wu_ref, wd_ref, gf_ref, o_ref,
                h_scr, comb_scr, acc_scr, *, final_norm):
    e = pl.program_id(1)

    @pl.when(e == 0)
    def _norm_and_route():
        x = x_ref[...]
        modv = mod_ref[0, 0]
        sh2, sc2 = modv[3:4], modv[4:5]
        ms = jnp.mean(x * x, axis=-1, keepdims=True)
        h = (x * lax.rsqrt(ms + EPS)) * g2_ref[...] * (1.0 + sc2) + sh2
        h_hi = h.astype(BF16)
        h_scr[...] = h_hi
        h_lo = (h - h_hi.astype(F32)).astype(BF16)
        wr = wr_ref[...]
        w_hi = wr.astype(BF16)
        w_lo = (wr - w_hi.astype(F32)).astype(BF16)
        logits = (jnp.dot(h_hi, w_hi, preferred_element_type=F32)
                  + jnp.dot(h_lo, w_hi, preferred_element_type=F32)
                  + jnp.dot(h_hi, w_lo, preferred_element_type=F32))
        scores = _sigmoid(logits)
        comb_scr[...] = _route(scores, scores + br_ref[...])
        acc_scr[...] = jnp.zeros_like(acc_scr)

    hb = h_scr[...]
    hg = jnp.dot(hb, wg_ref[0], preferred_element_type=F32)
    hu = jnp.dot(hb, wu_ref[0], preferred_element_type=F32)
    y = _bdot(_silu(hg) * hu, wd_ref[0])
    comb = comb_scr[...]
    lane = lax.broadcasted_iota(jnp.int32, comb.shape, 1)
    ce = jnp.sum(jnp.where(lane == e, comb, 0.0), axis=-1, keepdims=True)
    acc_scr[...] += ce * y

    @pl.when(e == pl.num_programs(1) - 1)
    def _residual():
        gt2 = mod_ref[0, 0][5:6]
        out = x_ref[...] + gt2 * acc_scr[...]
        if final_norm:
            ms = jnp.mean(out * out, axis=-1, keepdims=True)
            out = out * lax.rsqrt(ms + EPS) * gf_ref[...]
        o_ref[...] = out


def _moe(x2d, mod, g2, wr_pad, br_pad, w_gate, w_up, w_down, g_final, *, layer, seq, final_norm):
    n_tok, d = x2d.shape
    n_exp, _, d_exp = w_gate.shape
    tm = MOE_TILE
    kern = functools.partial(_moe_kernel, final_norm=final_norm)
    return pl.pallas_call(
        kern,
        grid=(n_tok // tm, n_exp),
        in_specs=[
            pl.BlockSpec((tm, d), lambda i, e: (i, 0)),
            pl.BlockSpec((1, 1, 6, d), lambda i, e: (layer, (i * tm) // seq, 0, 0)),
            pl.BlockSpec((1, d), lambda i, e: (0, 0)),
            pl.BlockSpec((d, LANES), lambda i, e: (0, 0)),
            pl.BlockSpec((1, LANES), lambda i, e: (0, 0)),
            pl.BlockSpec((1, d, d_exp), lambda i, e: (e, 0, 0)),
            pl.BlockSpec((1, d, d_exp), lambda i, e: (e, 0, 0)),
            pl.BlockSpec((1, d_exp, d), lambda i, e: (e, 0, 0)),
            pl.BlockSpec((1, d), lambda i, e: (0, 0)),
        ],
        out_specs=pl.BlockSpec((tm, d), lambda i, e: (i, 0)),
        out_shape=jax.ShapeDtypeStruct((n_tok, d), F32),
        scratch_shapes=[
            pltpu.VMEM((tm, d), BF16),
            pltpu.VMEM((tm, LANES), F32),
            pltpu.VMEM((tm, d), F32),
        ],
        compiler_params=pltpu.CompilerParams(
            dimension_semantics=("arbitrary", "arbitrary"),
            vmem_limit_bytes=VMEM_LIMIT_BYTES),
        name=f"moe_layer{layer}",
    )(x2d, mod, g2, wr_pad, br_pad, w_gate, w_up, w_down, g_final)


def _rope_tables(seq):
    half = RB_DK // 2
    inv = ROPE_BASE ** (-jnp.arange(half, dtype=F32) / half)
    ang = jnp.arange(seq, dtype=F32)[:, None] * inv[None, :]
    cos, sin = jnp.cos(ang), jnp.sin(ang)
    return jnp.concatenate([cos, cos], axis=-1), jnp.concatenate([-sin, sin], axis=-1)


def kernel(x, c, w_ada, b_ada, g_norm1, g_norm2, w_in, lb_logits, g_hgrn, g_ret, w_branch_a,
           w_branch_b, w_out, w_router, b_router, w_exp_gate, w_exp_up, w_exp_down, g_final):
    bsz, seq, d = x.shape
    depth = w_ada.shape[0]
    assert seq % MIX_TILE == 0 and MIX_TILE % CHUNK == 0 and seq % MOE_TILE == 0
    assert w_ada.shape[2] == 6 * d and w_ada.shape[2] % MOD_COLS == 0

    mod = _modulation(c, w_ada, b_ada).reshape(depth, bsz, 6, d)
    cos_t, sin_t = _rope_tables(seq)
    wr_pad = jnp.pad(w_router, ((0, 0), (0, LANES - N_EXPERTS)))
    br_pad = jnp.pad(b_router.astype(F32), (0, LANES - N_EXPERTS)).reshape(1, LANES)
    lbl = lb_logits.astype(F32)

    for l in range(depth):
        x = _mixer(x, mod, g_norm1[l].reshape(1, d), w_in[l].astype(BF16), lbl,
                   g_hgrn[l].reshape(1, HA_V), g_ret[l].reshape(1, RB_V),
                   w_branch_a[l].astype(BF16), w_branch_b[l].astype(BF16),
                   w_out[l].astype(BF16), cos_t, sin_t, layer=l)
        x = _moe(x.reshape(bsz * seq, d), mod, g_norm2[l].reshape(1, d), wr_pad, br_pad,
                 w_exp_gate[l].astype(BF16), w_exp_up[l].astype(BF16),
                 w_exp_down[l].astype(BF16), g_final.reshape(1, d),
                 layer=l, seq=seq, final_norm=(l == depth - 1)).reshape(bsz, seq, d)
    return x
```

```python
import functools
import math

import jax
import jax.numpy as jnp
from jax import lax
from jax.experimental import pallas as pl
from jax.experimental.pallas import tpu as pltpu

F32 = jnp.float32
BF16 = jnp.bfloat16

EPS = 1e-6
CHUNK = 64
ROPE_BASE = 10000.0
HA_HEADS, HA_DK, HA_DV = 4, 128, 128
RB_HEADS, RB_DK, RB_DV = 4, 128, 256
HA_K, HA_V = HA_HEADS * HA_DK, HA_HEADS * HA_DV
RB_K, RB_V = RB_HEADS * RB_DK, RB_HEADS * RB_DV
N_EXPERTS, N_GROUPS, TOP_K = 16, 4, 2
EXP_PER_GROUP = N_EXPERTS // N_GROUPS

LANES = 128
SUBLANES = 8
VMEM_LIMIT_BYTES = 56 * 1024 * 1024

MIX_TILE = 256
ROUTE_TILE = 1024
DISPATCH_TILE = 2048
SORT_TILE = 512
COMBINE_TILE = 1024
MOD_COLS = 1536

PAIRS = tuple((a, b) for a in range(EXP_PER_GROUP) for b in range(a + 1, EXP_PER_GROUP))
N_PAIRS = len(PAIRS)
BUCKET_ROWS = 32

_NT = (((1,), (1,)), ((), ()))
_TN = (((0,), (0,)), ((), ()))


def _sigmoid(v):
    return jax.nn.sigmoid(v)


def _silu(v):
    return v * jax.nn.sigmoid(v)


def _bdot(a, b, dims=None):
    a = a.astype(BF16)
    b = b.astype(BF16)
    if dims is None:
        return jnp.dot(a, b, preferred_element_type=F32)
    return lax.dot_general(a, b, dims, preferred_element_type=F32)


def _mod_kernel(c_ref, w_ref, b_ref, o_ref):
    o_ref[0] = _bdot(_silu(c_ref[...]), w_ref[0]) + b_ref[0]


def _modulation(c, w_ada, b_ada):
    depth, d, six_d = w_ada.shape
    bsz = c.shape[0]
    return pl.pallas_call(
        _mod_kernel,
        grid=(depth, six_d // MOD_COLS),
        in_specs=[
            pl.BlockSpec((bsz, d), lambda l, j: (0, 0)),
            pl.BlockSpec((1, d, MOD_COLS), lambda l, j: (l, 0, j)),
            pl.BlockSpec((1, 1, MOD_COLS), lambda l, j: (l, 0, j)),
        ],
        out_specs=pl.BlockSpec((1, bsz, MOD_COLS), lambda l, j: (l, 0, j)),
        out_shape=jax.ShapeDtypeStruct((depth, bsz, six_d), F32),
        compiler_params=pltpu.CompilerParams(
            dimension_semantics=("arbitrary", "arbitrary"),
            vmem_limit_bytes=VMEM_LIMIT_BYTES),
        name="adaln_modulation",
    )(c, w_ada, b_ada.reshape(depth, 1, six_d))


def _level_ref(cum_scr, cumh, cols, block):
    t = cumh.shape[0]
    half = block // 2
    if block >= SUBLANES:
        pieces = []
        for m in range(t // block):
            r = m * block + half - 1
            pieces.append(jnp.broadcast_to(cum_scr[r:r + 1, cols], (block, cumh.shape[1])))
        return pieces[0] if len(pieces) == 1 else jnp.concatenate(pieces, axis=0)
    pos = lax.broadcasted_iota(jnp.int32, cumh.shape, 0) & (block - 1)
    ref = cumh
    for off in range(-half, half):
        if off == 0:
            continue
        shifted = pltpu.roll(cumh, (-off) % t, 0)
        ref = jnp.where(pos == half - 1 - off, shifted, ref)
    return ref


def _mixer_kernel(x_ref, mod_ref, g1_ref, win_ref, lbl_ref, ghg_ref, gret_ref,
                  wa_ref, wb_ref, wo_ref, cos_ref, sin_ref, o_ref,
                  z_scr, sa_scr, sb_scr, cum_scr, k_scr, retw_scr, lvl_scr, tri_scr,
                  oa_scr, ob_scr, *, layer, depth):
    t = x_ref.shape[1]
    n_levels = t.bit_length() - 1
    step = pl.program_id(1)

    @pl.when(jnp.logical_and(pl.program_id(0) == 0, step == 0))
    def _build_constants():
        row = lax.broadcasted_iota(jnp.int32, (t, t), 0)
        col = lax.broadcasted_iota(jnp.int32, (t, t), 1)
        diff = row ^ col
        lvl = jnp.zeros((t, t), jnp.int32)
        for bit in range(n_levels):
            lvl = lvl + (diff >= (1 << bit)).astype(jnp.int32)
        lvl_scr[...] = jnp.where(col > row, -1, lvl)
        tri_scr[...] = (col <= row).astype(BF16)
        dist = jnp.abs(row - col).astype(F32)
        visible = (col // CHUNK) <= (row // CHUNK)
        for h in range(RB_HEADS):
            log_gamma = math.log(1.0 - 2.0 ** (-5.0 - h))
            retw_scr[h] = jnp.where(visible, jnp.exp(log_gamma * dist), 0.0)

    @pl.when(step == 0)
    def _reset_state():
        sa_scr[...] = jnp.zeros_like(sa_scr)
        sb_scr[...] = jnp.zeros_like(sb_scr)

    x = x_ref[0]
    modv = mod_ref[0, 0]
    sh1, sc1, gt1 = modv[0:1], modv[1:2], modv[2:3]
    ms = jnp.mean(x * x, axis=-1, keepdims=True)
    hn = (x * lax.rsqrt(ms + EPS)) * g1_ref[...]
    hb = (hn * (1.0 + sc1) + sh1).astype(BF16)
    n_in = win_ref.shape[1]
    seg = 1024
    for s in range(n_in // seg):
        z_scr[:, s * seg:(s + 1) * seg] = jnp.dot(
            hb, win_ref[:, s * seg:(s + 1) * seg], preferred_element_type=F32)

    rows = [lbl_ref[i:i + 1, :] for i in range(depth)]
    mx = functools.reduce(jnp.maximum, rows)
    ex = [jnp.exp(r - mx) for r in rows]
    tot = functools.reduce(lambda a, b: a + b, ex)
    lb = functools.reduce(lambda a, b: a + b, [e / tot for e in ex[:layer + 1]]) - ex[0] / tot

    o_q, o_f, o_i, o_og = 0, HA_K, 2 * HA_K, 2 * HA_K + HA_V
    o_qb = o_og + HA_V
    o_kb = o_qb + RB_K
    o_vb = o_kb + RB_K
    o_ogb = o_vb + RB_V
    o_ma = o_ogb + RB_V
    o_mb = o_ma + x.shape[1]

    f = lb + (1.0 - lb) * _sigmoid(z_scr[:, o_f:o_f + HA_K])
    k_scr[...] = 1.0 - f
    g = jnp.log(f)
    g_hi = g.astype(BF16)
    g_lo = (g - g_hi.astype(F32)).astype(BF16)
    tri = tri_scr[...]
    cum_scr[...] = (jnp.dot(tri, g_hi, preferred_element_type=F32)
                    + jnp.dot(tri, g_lo, preferred_element_type=F32))

    lvl = lvl_scr[...]
    for h in range(HA_HEADS):
        cols = slice(h * HA_DK, (h + 1) * HA_DK)
        q = _silu(z_scr[:, o_q + h * HA_DK:o_q + (h + 1) * HA_DK])
        k = k_scr[:, cols]
        v = z_scr[:, o_i + h * HA_DV:o_i + (h + 1) * HA_DV].astype(BF16)
        cumh = cum_scr[:, cols]
        scores = jnp.zeros((t, t), F32)
        for level in range(n_levels + 1):
            if level == 0:
                qs, ks = q, k
            else:
                ref = _level_ref(cum_scr, cumh, cols, 1 << level)
                e = jnp.exp(-jnp.abs(cumh - ref))
                qs, ks = q * e, k * e
            scores = jnp.where(lvl == level, _bdot(qs, ks, _NT), scores)
        st = sa_scr[h]
        last = cum_scr[t - 1:t, cols]
        o = _bdot(scores, v) + _bdot(q * jnp.exp(cumh), st, _NT)
        sa_scr[h] = st * jnp.exp(last) + _bdot(v, k * jnp.exp(last - cumh), _TN)
        on = o * lax.rsqrt(jnp.mean(o * o, axis=-1, keepdims=True) + EPS) * ghg_ref[:, cols]
        og = z_scr[:, o_og + h * HA_DV:o_og + (h + 1) * HA_DV]
        oa_scr[:, cols] = (on * _silu(og)).astype(BF16)

    cosf = cos_ref[...]
    sins = sin_ref[...]
    pos = lax.broadcasted_iota(jnp.int32, (t, RB_DK), 0).astype(F32)
    for h in range(RB_HEADS):
        log_gamma = math.log(1.0 - 2.0 ** (-5.0 - h))
        qb = z_scr[:, o_qb + h * RB_DK:o_qb + (h + 1) * RB_DK]
        kb = z_scr[:, o_kb + h * RB_DK:o_kb + (h + 1) * RB_DK]
        qr = (qb * cosf + pltpu.roll(qb, RB_DK // 2, 1) * sins) * (RB_DK ** -0.5)
        kr = kb * cosf + pltpu.roll(kb, RB_DK // 2, 1) * sins
        vcols = slice(h * RB_DV, (h + 1) * RB_DV)
        vb = z_scr[:, o_vb + h * RB_DV:o_vb + (h + 1) * RB_DV].astype(BF16)
        s = _bdot(qr, kr, _NT) * retw_scr[h]
        sb = sb_scr[h]
        q_dec = jnp.exp(log_gamma * (pos + 1.0))
        k_dec = jnp.exp(log_gamma * (float(t - 1) - pos))
        o = _bdot(s, vb) + _bdot(qr * q_dec, sb)
        sb_scr[h] = math.exp(log_gamma * t) * sb + _bdot(kr * k_dec, vb, _TN)
        mu = jnp.mean(o, axis=-1, keepdims=True)
        oc = o - mu
        var = jnp.mean(oc * oc, axis=-1, keepdims=True)
        on = oc * lax.rsqrt(var + EPS) * gret_ref[:, vcols]
        og = z_scr[:, o_ogb + h * RB_DV:o_ogb + (h + 1) * RB_DV]
        ob_scr[:, vcols] = (on * _silu(og)).astype(BF16)

    d = x.shape[1]
    ya = jnp.dot(oa_scr[...], wa_ref[...], preferred_element_type=F32)
    yb = jnp.dot(ob_scr[...], wb_ref[...], preferred_element_type=F32)
    y = _sigmoid(z_scr[:, o_ma:o_ma + d]) * ya + _sigmoid(z_scr[:, o_mb:o_mb + d]) * yb
    o_ref[0] = x + gt1 * _bdot(y, wo_ref[...])


def _const_spec(shape):
    return pl.BlockSpec(shape, lambda b, s: (0,) * len(shape), pipeline_mode=pl.Buffered(1))


def _mixer(x, mod, g1, w_in, lb_logits, g_hgrn, g_ret, w_a, w_b, w_o, cos_t, sin_t, *, layer):
    bsz, seq, d = x.shape
    depth = mod.shape[0]
    t = MIX_TILE
    n_in = w_in.shape[1]
    kern = functools.partial(_mixer_kernel, layer=layer, depth=depth)
    return pl.pallas_call(
        kern,
        grid=(bsz, seq // t),
        in_specs=[
            pl.BlockSpec((1, t, d), lambda b, s: (b, s, 0)),
            pl.BlockSpec((1, 1, 6, d), lambda b, s: (layer, b, 0, 0)),
            _const_spec((1, d)),
            _const_spec((d, n_in)),
            _const_spec((depth, HA_K)),
            _const_spec((1, HA_V)),
            _const_spec((1, RB_V)),
            _const_spec((HA_V, d)),
            _const_spec((RB_V, d)),
            _const_spec((d, d)),
            pl.BlockSpec((t, RB_DK), lambda b, s: (s, 0)),
            pl.BlockSpec((t, RB_DK), lambda b, s: (s, 0)),
        ],
        out_specs=pl.BlockSpec((1, t, d), lambda b, s: (b, s, 0)),
        out_shape=jax.ShapeDtypeStruct((bsz, seq, d), F32),
        scratch_shapes=[
            pltpu.VMEM((t, n_in), F32),
            pltpu.VMEM((HA_HEADS, HA_DV, HA_DK), F32),
            pltpu.VMEM((RB_HEADS, RB_DK, RB_DV), F32),
            pltpu.VMEM((t, HA_K), F32),
            pltpu.VMEM((t, HA_K), F32),
            pltpu.VMEM((RB_HEADS, t, t), F32),
            pltpu.VMEM((t, t), jnp.int32),
            pltpu.VMEM((t, t), BF16),
            pltpu.VMEM((t, HA_V), BF16),
            pltpu.VMEM((t, RB_V), BF16),
        ],
        compiler_params=pltpu.CompilerParams(
            dimension_semantics=("arbitrary", "arbitrary"),
            vmem_limit_bytes=VMEM_LIMIT_BYTES),
        name=f"mixer_layer{layer}",
    )(x, mod, g1, w_in, lb_logits, g_hgrn, g_ret, w_a, w_b, w_o, cos_t, sin_t)


def _lane_shift(a, delta):
    return pltpu.roll(a, (-delta) % a.shape[1], 1)


def _route(scores, biased):
    lane = lax.broadcasted_iota(jnp.int32, scores.shape, 1)
    pos = lane & (EXP_PER_GROUP - 1)
    grp = lane // EXP_PER_GROUP
    rank = jnp.zeros(scores.shape, jnp.int32)
    for delta in range(-(EXP_PER_GROUP - 1), EXP_PER_GROUP):
        if delta == 0:
            continue
        other = _lane_shift(biased, delta)
        ahead = (other > biased) | ((other == biased) & (delta < 0))
        in_grp = (pos + delta >= 0) & (pos + delta < EXP_PER_GROUP)
        rank = rank + (ahead & in_grp).astype(jnp.int32)
    top = rank < TOP_K
    kept = jnp.where(top, biased, 0.0)
    grp_score = kept
    for delta in range(-(EXP_PER_GROUP - 1), EXP_PER_GROUP):
        if delta == 0:
            continue
        in_grp = (pos + delta >= 0) & (pos + delta < EXP_PER_GROUP)
        grp_score = grp_score + jnp.where(in_grp, _lane_shift(kept, delta), 0.0)
    grank = jnp.zeros(scores.shape, jnp.int32)
    for dg in range(-(N_GROUPS - 1), N_GROUPS):
        if dg == 0:
            continue
        other = _lane_shift(grp_score, dg * EXP_PER_GROUP)
        ahead = (other > grp_score) | ((other == grp_score) & (dg < 0))
        in_rng = (grp + dg >= 0) & (grp + dg < N_GROUPS)
        grank = grank + (ahead & in_rng).astype(jnp.int32)
    sel = top & (grank == 0) & (lane < N_EXPERTS)
    w = jnp.where(sel, scores, 0.0)
    return w / jnp.sum(w, axis=-1, keepdims=True), sel


def _router_kernel(x_ref, mod_ref, g2_ref, wr_ref, br_ref, p_ref, idx_ref, cnt_ref,
                   before_scr, run_scr):
    tm, d = x_ref.shape
    half = d // 2
    i = pl.program_id(0)

    @pl.when(i == 0)
    def _init():
        row = lax.broadcasted_iota(jnp.int32, (tm, tm), 0)
        col = lax.broadcasted_iota(jnp.int32, (tm, tm), 1)
        before_scr[...] = (row < col).astype(BF16)
        run_scr[...] = jnp.zeros_like(run_scr)

    x = x_ref[...]
    modv = mod_ref[0, 0]
    sh2, sc2 = modv[3:4], modv[4:5]
    ms = jnp.mean(x * x, axis=-1, keepdims=True)
    h = (x * lax.rsqrt(ms + EPS)) * g2_ref[...] * (1.0 + sc2) + sh2
    h_hi = h.astype(BF16)
    h_hi32 = h_hi.astype(F32)
    h_lo = (h - h_hi32).astype(BF16)
    wr = wr_ref[...]
    w_hi = wr.astype(BF16)
    w_lo = (wr - w_hi.astype(F32)).astype(BF16)
    logits = (jnp.dot(h_hi, w_hi, preferred_element_type=F32)
              + jnp.dot(h_lo, w_hi, preferred_element_type=F32)
              + jnp.dot(h_hi, w_lo, preferred_element_type=F32))
    scores = _sigmoid(logits)
    comb, sel = _route(scores, scores + br_ref[...])

    bits = pltpu.bitcast(h_hi32, jnp.uint32)
    p_ref[:, 0:half] = (bits[:, 0:half] >> 16) | bits[:, half:d]
    p_ref[:, half:half + LANES] = pltpu.bitcast(comb, jnp.uint32)

    sel_t = sel.astype(F32).T
    rows = [sel_t[e:e + 1, :] for e in range(N_EXPERTS)]
    bucket = jnp.zeros((1, tm), F32)
    for g in range(N_GROUPS):
        grp_rows = rows[g * EXP_PER_GROUP:(g + 1) * EXP_PER_GROUP]
        bucket = bucket + float(g * N_PAIRS) * functools.reduce(jnp.maximum, grp_rows)
        for p, (a, b) in enumerate(PAIRS):
            if p:
                bucket = bucket + float(p) * (grp_rows[a] * grp_rows[b])
    bucket = bucket.astype(jnp.int32)
    onehot = (lax.broadcasted_iota(jnp.int32, (BUCKET_ROWS, tm), 0) == bucket).astype(F32)
    before = jnp.dot(onehot.astype(BF16), before_scr[...], preferred_element_type=F32)
    run = run_scr[...]
    rank = jnp.sum(onehot * (before + run[:, 0:1]), axis=0, keepdims=True)
    run = run + jnp.sum(onehot, axis=1, keepdims=True)
    run_scr[...] = run
    idx_ref[0] = jnp.concatenate(
        [bucket, rank.astype(jnp.int32), jnp.zeros((SUBLANES - 2, tm), jnp.int32)], axis=0)
    cnt_ref[...] = run.astype(jnp.int32)


def _router(x2d, mod, g2, wr_pad, br_pad, *, layer, seq):
    n_tok, d = x2d.shape
    tm = ROUTE_TILE
    n_tiles = n_tok // tm
    row_w = d // 2 + LANES
    return pl.pallas_call(
        _router_kernel,
        grid=(n_tiles,),
        in_specs=[
            pl.BlockSpec((tm, d), lambda i: (i, 0)),
            pl.BlockSpec((1, 1, 6, d), lambda i: (layer, (i * tm) // seq, 0, 0)),
            pl.BlockSpec((1, d), lambda i: (0, 0)),
            pl.BlockSpec((d, LANES), lambda i: (0, 0)),
            pl.BlockSpec((1, LANES), lambda i: (0, 0)),
        ],
        out_specs=[
            pl.BlockSpec((tm, row_w), lambda i: (i, 0)),
            pl.BlockSpec((1, SUBLANES, tm), lambda i: (i, 0, 0)),
            pl.BlockSpec((BUCKET_ROWS, LANES), lambda i: (0, 0)),
        ],
        out_shape=[
            jax.ShapeDtypeStruct((n_tok, row_w), jnp.uint32),
            jax.ShapeDtypeStruct((n_tiles, SUBLANES, tm), jnp.int32),
            jax.ShapeDtypeStruct((BUCKET_ROWS, LANES), jnp.int32),
        ],
        scratch_shapes=[
            pltpu.VMEM((tm, tm), BF16),
            pltpu.VMEM((BUCKET_ROWS, LANES), F32),
        ],
        compiler_params=pltpu.CompilerParams(
            dimension_semantics=("arbitrary",), vmem_limit_bytes=VMEM_LIMIT_BYTES),
        name=f"moe_router{layer}",
    )(x2d, mod, g2, wr_pad, br_pad)


def _dispatch_kernel(dest_ref, p_ref, init_ref, hs_ref, sem):
    del init_ref
    td = p_ref.shape[0]

    def issue(r, carry):
        pltpu.make_async_copy(p_ref.at[pl.ds(r, 1), :],
                              hs_ref.at[pl.ds(dest_ref[0, 0, r], 1), :], sem).start()
        return carry

    lax.fori_loop(0, td, issue, 0, unroll=8)

    def drain(r, carry):
        pltpu.make_async_copy(p_ref.at[pl.ds(0, 1), :], hs_ref.at[pl.ds(0, 1), :], sem).wait()
        return carry

    lax.fori_loop(0, td, drain, 0, unroll=8)


def _dispatch(p, dest2d, n_sorted, *, layer):
    n_tok, row_w = p.shape
    td = dest2d.shape[2]
    init = jnp.zeros((n_sorted, row_w), jnp.uint32)
    return pl.pallas_call(
        _dispatch_kernel,
        grid=(n_tok // td,),
        in_specs=[
            pl.BlockSpec((1, 1, td), lambda i: (i, 0, 0), memory_space=pltpu.SMEM),
            pl.BlockSpec((td, row_w), lambda i: (i, 0)),
            pl.BlockSpec(memory_space=pl.ANY),
        ],
        out_specs=pl.BlockSpec(memory_space=pl.ANY),
        out_shape=jax.ShapeDtypeStruct((n_sorted, row_w), jnp.uint32),
        scratch_shapes=[pltpu.SemaphoreType.DMA(())],
        input_output_aliases={2: 0},
        compiler_params=pltpu.CompilerParams(
            dimension_semantics=("arbitrary",), vmem_limit_bytes=VMEM_LIMIT_BYTES),
        name=f"moe_dispatch{layer}",
    )(dest2d, p, init)


def _expert_kernel(e1_ref, e2_ref, valid_ref, hs_ref, wg1_ref, wu1_ref, wd1_ref,
                   wg2_ref, wu2_ref, wd2_ref, y_ref):
    j = pl.program_id(0)
    half = hs_ref.shape[1] - LANES

    @pl.when(valid_ref[j] != 0)
    def _compute():
        packed = hs_ref[:, 0:half]
        lo = pltpu.bitcast(packed << 16, F32)
        hi = pltpu.bitcast(packed & jnp.uint32(0xFFFF0000), F32)
        h = jnp.concatenate([lo, hi], axis=1).astype(BF16)
        comb = pltpu.bitcast(hs_ref[:, half:half + LANES], F32)
        lane = lax.broadcasted_iota(jnp.int32, comb.shape, 1)

        def expert(e, wg_ref, wu_ref, wd_ref):
            ce = jnp.sum(jnp.where(lane == e, comb, 0.0), axis=-1, keepdims=True)
            hg = jnp.dot(h, wg_ref[0], preferred_element_type=F32)
            hu = jnp.dot(h, wu_ref[0], preferred_element_type=F32)
            return ce * _bdot(_silu(hg) * hu, wd_ref[0])

        y_ref[...] = (expert(e1_ref[j], wg1_ref, wu1_ref, wd1_ref)
                      + expert(e2_ref[j], wg2_ref, wu2_ref, wd2_ref))

    @pl.when(valid_ref[j] == 0)
    def _empty():
        y_ref[...] = jnp.zeros_like(y_ref)


def _experts(hs, e1, e2, valid, w_gate, w_up, w_down, *, layer):
    n_sorted, row_w = hs.shape
    _, d, d_exp = w_gate.shape
    ts = SORT_TILE
    first = lambda j, e1, e2, valid: (e1[j], 0, 0)
    second = lambda j, e1, e2, valid: (e2[j], 0, 0)
    grid_spec = pltpu.PrefetchScalarGridSpec(
        num_scalar_prefetch=3,
        grid=(n_sorted // ts,),
        in_specs=[
            pl.BlockSpec((ts, row_w), lambda j, e1, e2, valid: (j, 0)),
            pl.BlockSpec((1, d, d_exp), first),
            pl.BlockSpec((1, d, d_exp), first),
            pl.BlockSpec((1, d_exp, d), first),
            pl.BlockSpec((1, d, d_exp), second),
            pl.BlockSpec((1, d, d_exp), second),
            pl.BlockSpec((1, d_exp, d), second),
        ],
        out_specs=pl.BlockSpec((ts, d), lambda j, e1, e2, valid: (j, 0)),
    )
    return pl.pallas_call(
        _expert_kernel,
        grid_spec=grid_spec,
        out_shape=jax.ShapeDtypeStruct((n_sorted, d), F32),
        compiler_params=pltpu.CompilerParams(
            dimension_semantics=("arbitrary",), vmem_limit_bytes=VMEM_LIMIT_BYTES),
        name=f"moe_experts{layer}",
    )(e1, e2, valid, hs, w_gate, w_up, w_down, w_gate, w_up, w_down)


def _combine_kernel(dest_ref, x_ref, mod_ref, gf_ref, ys_ref, o_ref, y_scr, sem, *, final_norm):
    tc = x_ref.shape[0]

    def issue(r, carry):
        pltpu.make_async_copy(ys_ref.at[pl.ds(dest_ref[0, 0, r], 1), :],
                              y_scr.at[pl.ds(r, 1), :], sem).start()
        return carry

    lax.fori_loop(0, tc, issue, 0, unroll=8)

    def drain(r, carry):
        pltpu.make_async_copy(ys_ref.at[pl.ds(0, 1), :], y_scr.at[pl.ds(0, 1), :], sem).wait()
        return carry

    lax.fori_loop(0, tc, drain, 0, unroll=8)

    gt2 = mod_ref[0, 0][5:6]
    out = x_ref[...] + gt2 * y_scr[...]
    if final_norm:
        ms = jnp.mean(out * out, axis=-1, keepdims=True)
        out = out * lax.rsqrt(ms + EPS) * gf_ref[...]
    o_ref[...] = out


def _combine(x2d, mod, g_final, ys, dest2d, *, layer, seq, final_norm):
    n_tok, d = x2d.shape
    tc = dest2d.shape[2]
    kern = functools.partial(_combine_kernel, final_norm=final_norm)
    return pl.pallas_call(
        kern,
        grid=(n_tok // tc,),
        in_specs=[
            pl.BlockSpec((1, 1, tc), lambda i: (i, 0, 0), memory_space=pltpu.SMEM),
            pl.BlockSpec((tc, d), lambda i: (i, 0)),
            pl.BlockSpec((1, 1, 6, d), lambda i: (layer, (i * tc) // seq, 0, 0)),
            pl.BlockSpec((1, d), lambda i: (0, 0)),
            pl.BlockSpec(memory_space=pl.ANY),
        ],
        out_specs=pl.BlockSpec((tc, d), lambda i: (i, 0)),
        out_shape=jax.ShapeDtypeStruct((n_tok, d), F32),
        scratch_shapes=[pltpu.VMEM((tc, d), F32), pltpu.SemaphoreType.DMA(())],
        compiler_params=pltpu.CompilerParams(
            dimension_semantics=("arbitrary",), vmem_limit_bytes=VMEM_LIMIT_BYTES),
        name=f"moe_combine{layer}",
    )(dest2d, x2d, mod, g_final, ys)


def _moe(x2d, mod, g2, wr_pad, br_pad, w_gate, w_up, w_down, g_final, *, layer, seq, final_norm):
    n_tok, _ = x2d.shape
    p, idx, cnt = _router(x2d, mod, g2, wr_pad, br_pad, layer=layer, seq=seq)

    n_buckets = N_GROUPS * N_PAIRS
    counts = cnt[:n_buckets, 0]
    padded = ((counts + SORT_TILE - 1) // SORT_TILE) * SORT_TILE
    ends = jnp.cumsum(padded)
    starts = ends - padded
    dest = starts[idx[:, 0, :].reshape(n_tok)] + idx[:, 1, :].reshape(n_tok)
    n_tiles = n_tok // SORT_TILE + n_buckets
    tile_start = jnp.arange(n_tiles, dtype=jnp.int32) * SORT_TILE
    tile_bucket = jnp.sum((ends[None, :] <= tile_start[:, None]).astype(jnp.int32), axis=1)
    valid = (tile_bucket < n_buckets).astype(jnp.int32)
    tb = jnp.minimum(tile_bucket, n_buckets - 1)
    grp, pair = tb // N_PAIRS, tb % N_PAIRS
    pair_lo = jnp.asarray([a for a, _ in PAIRS], jnp.int32)
    pair_hi = jnp.asarray([b for _, b in PAIRS], jnp.int32)
    e1 = grp * EXP_PER_GROUP + pair_lo[pair]
    e2 = grp * EXP_PER_GROUP + pair_hi[pair]

    hs = _dispatch(p, dest.reshape(n_tok // DISPATCH_TILE, 1, DISPATCH_TILE),
                   n_tiles * SORT_TILE, layer=layer)
    ys = _experts(hs, e1, e2, valid, w_gate, w_up, w_down, layer=layer)
    return _combine(x2d, mod, g_final, ys, dest.reshape(n_tok // COMBINE_TILE, 1, COMBINE_TILE),
                    layer=layer, seq=seq, final_norm=final_norm)


def _rope_tables(seq):
    half = RB_DK // 2
    inv = ROPE_BASE ** (-jnp.arange(half, dtype=F32) / half)
    ang = jnp.arange(seq, dtype=F32)[:, None] * inv[None, :]
    cos, sin = jnp.cos(ang), jnp.sin(ang)
    return jnp.concatenate([cos, cos], axis=-1), jnp.concatenate([-sin, sin], axis=-1)


def kernel(x, c, w_ada, b_ada, g_norm1, g_norm2, w_in, lb_logits, g_hgrn, g_ret, w_branch_a,
           w_branch_b, w_out, w_router, b_router, w_exp_gate, w_exp_up, w_exp_down, g_final):
    bsz, seq, d = x.shape
    depth = w_ada.shape[0]
    assert seq % MIX_TILE == 0 and MIX_TILE % CHUNK == 0
    assert seq % ROUTE_TILE == 0 and seq % COMBINE_TILE == 0
    assert (bsz * seq) % DISPATCH_TILE == 0 and (bsz * seq) % SORT_TILE == 0
    assert w_ada.shape[2] == 6 * d and w_ada.shape[2] % MOD_COLS == 0
    assert N_GROUPS * N_PAIRS <= BUCKET_ROWS and TOP_K == 2

    mod = _modulation(c, w_ada, b_ada).reshape(depth, bsz, 6, d)
    cos_t, sin_t = _rope_tables(seq)
    wr_pad = jnp.pad(w_router, ((0, 0), (0, LANES - N_EXPERTS)))
    br_pad = jnp.pad(b_router.astype(F32), (0, LANES - N_EXPERTS)).reshape(1, LANES)
    lbl = lb_logits.astype(F32)

    for l in range(depth):
        x = _mixer(x, mod, g_norm1[l].reshape(1, d), w_in[l].astype(BF16), lbl,
                   g_hgrn[l].reshape(1, HA_V), g_ret[l].reshape(1, RB_V),
                   w_branch_a[l].astype(BF16), w_branch_b[l].astype(BF16),
                   w_out[l].astype(BF16), cos_t, sin_t, layer=l)
        x = _moe(x.reshape(bsz * seq, d), mod, g_norm2[l].reshape(1, d), wr_pad, br_pad,
                 w_exp_gate[l].astype(BF16), w_exp_up[l].astype(BF16),
                 w_exp_down[l].astype(BF16), g_final.reshape(1, d),
                 layer=l, seq=seq, final_norm=(l == depth - 1)).reshape(bsz, seq, d)
    return x
```

```python
import functools
import math

import jax
import jax.numpy as jnp
from jax import lax
from jax.experimental import pallas as pl
from jax.experimental.pallas import tpu as pltpu

F32 = jnp.float32
BF16 = jnp.bfloat16

EPS = 1e-6
CHUNK = 64
ROPE_BASE = 10000.0
HA_HEADS, HA_DK, HA_DV = 4, 128, 128
RB_HEADS, RB_DK, RB_DV = 4, 128, 256
HA_K, HA_V = HA_HEADS * HA_DK, HA_HEADS * HA_DV
RB_K, RB_V = RB_HEADS * RB_DK, RB_HEADS * RB_DV
N_EXPERTS, N_GROUPS, TOP_K = 16, 4, 2
EXP_PER_GROUP = N_EXPERTS // N_GROUPS

LANES = 128
SUBLANES = 8
VMEM_LIMIT_BYTES = 56 * 1024 * 1024

MIX_TILE = 256
ROUTE_TILE = 1024
DISPATCH_TILE = 2048
SORT_TILE = 512
COMBINE_TILE = 1024
MOD_COLS = 1536
DMA_PRIORITIES = 2

PAIRS = tuple((a, b) for a in range(EXP_PER_GROUP) for b in range(a + 1, EXP_PER_GROUP))
N_PAIRS = len(PAIRS)
BUCKET_ROWS = 32

_NT = (((1,), (1,)), ((), ()))
_TN = (((0,), (0,)), ((), ()))


def _sigmoid(v):
    return jax.nn.sigmoid(v)


def _silu(v):
    return v * jax.nn.sigmoid(v)


def _bdot(a, b, dims=None):
    a = a.astype(BF16)
    b = b.astype(BF16)
    if dims is None:
        return jnp.dot(a, b, preferred_element_type=F32)
    return lax.dot_general(a, b, dims, preferred_element_type=F32)


def _mod_kernel(c_ref, w_ref, b_ref, o_ref):
    o_ref[0] = _bdot(_silu(c_ref[...]), w_ref[0]) + b_ref[0]


def _modulation(c, w_ada, b_ada):
    depth, d, six_d = w_ada.shape
    bsz = c.shape[0]
    return pl.pallas_call(
        _mod_kernel,
        grid=(depth, six_d // MOD_COLS),
        in_specs=[
            pl.BlockSpec((bsz, d), lambda l, j: (0, 0)),
            pl.BlockSpec((1, d, MOD_COLS), lambda l, j: (l, 0, j)),
            pl.BlockSpec((1, 1, MOD_COLS), lambda l, j: (l, 0, j)),
        ],
        out_specs=pl.BlockSpec((1, bsz, MOD_COLS), lambda l, j: (l, 0, j)),
        out_shape=jax.ShapeDtypeStruct((depth, bsz, six_d), F32),
        compiler_params=pltpu.CompilerParams(
            dimension_semantics=("arbitrary", "arbitrary"),
            vmem_limit_bytes=VMEM_LIMIT_BYTES),
        name="adaln_modulation",
    )(c, w_ada, b_ada.reshape(depth, 1, six_d))


def _level_ref(cum_scr, cumh, cols, block):
    t = cumh.shape[0]
    half = block // 2
    if block >= SUBLANES:
        pieces = []
        for m in range(t // block):
            r = m * block + half - 1
            pieces.append(jnp.broadcast_to(cum_scr[r:r + 1, cols], (block, cumh.shape[1])))
        return pieces[0] if len(pieces) == 1 else jnp.concatenate(pieces, axis=0)
    pos = lax.broadcasted_iota(jnp.int32, cumh.shape, 0) & (block - 1)
    ref = cumh
    for off in range(-half, half):
        if off == 0:
            continue
        shifted = pltpu.roll(cumh, (-off) % t, 0)
        ref = jnp.where(pos == half - 1 - off, shifted, ref)
    return ref


def _project(x_ref, mod_ref, g1_ref, win_ref, z_scr):
    x = x_ref[0]
    modv = mod_ref[0, 0]
    sh1, sc1 = modv[0:1], modv[1:2]
    ms = jnp.mean(x * x, axis=-1, keepdims=True)
    hn = (x * lax.rsqrt(ms + EPS)) * g1_ref[...]
    hb = (hn * (1.0 + sc1) + sh1).astype(BF16)
    yield
    n_in = win_ref.shape[1]
    seg = 1024
    for s in range(n_in // seg):
        z_scr[:, s * seg:(s + 1) * seg] = jnp.dot(
            hb, win_ref[:, s * seg:(s + 1) * seg], preferred_element_type=F32)
        yield


def _mix(z_scr, x_ref, mod_ref, first, lbl_ref, ghg_ref, gret_ref, wa_ref, wb_ref, wo_ref,
         cos_ref, sin_ref, o_ref, sa_scr, sb_scr, cum_scr, k_scr, retw_scr, lvl_scr, tri_scr,
         oa_scr, ob_scr, *, layer, depth):
    t = x_ref.shape[1]
    n_levels = t.bit_length() - 1
    x = x_ref[0]
    gt1 = mod_ref[0, 0][2:3]

    rows = [lbl_ref[i:i + 1, :] for i in range(depth)]
    mx = functools.reduce(jnp.maximum, rows)
    ex = [jnp.exp(r - mx) for r in rows]
    tot = functools.reduce(lambda a, b: a + b, ex)
    lb = functools.reduce(lambda a, b: a + b, [e / tot for e in ex[:layer + 1]]) - ex[0] / tot

    o_q, o_f, o_i, o_og = 0, HA_K, 2 * HA_K, 2 * HA_K + HA_V
    o_qb = o_og + HA_V
    o_kb = o_qb + RB_K
    o_vb = o_kb + RB_K
    o_ogb = o_vb + RB_V
    o_ma = o_ogb + RB_V
    o_mb = o_ma + x.shape[1]

    f = lb + (1.0 - lb) * _sigmoid(z_scr[:, o_f:o_f + HA_K])
    k_scr[...] = 1.0 - f
    g = jnp.log(f)
    g_hi = g.astype(BF16)
    g_lo = (g - g_hi.astype(F32)).astype(BF16)
    tri = tri_scr[...]
    cum_scr[...] = (jnp.dot(tri, g_hi, preferred_element_type=F32)
                    + jnp.dot(tri, g_lo, preferred_element_type=F32))

    yield

    lvl = lvl_scr[...]
    for h in range(HA_HEADS):
        cols = slice(h * HA_DK, (h + 1) * HA_DK)
        q = _silu(z_scr[:, o_q + h * HA_DK:o_q + (h + 1) * HA_DK])
        k = k_scr[:, cols]
        v = z_scr[:, o_i + h * HA_DV:o_i + (h + 1) * HA_DV].astype(BF16)
        cumh = cum_scr[:, cols]
        scores = jnp.zeros((t, t), F32)
        for level in range(n_levels + 1):
            if level == 0:
                qs, ks = q, k
            else:
                ref = _level_ref(cum_scr, cumh, cols, 1 << level)
                e = jnp.exp(-jnp.abs(cumh - ref))
                qs, ks = q * e, k * e
            scores = jnp.where(lvl == level, _bdot(qs, ks, _NT), scores)
        st = jnp.where(first, 0.0, sa_scr[h])
        last = cum_scr[t - 1:t, cols]
        o = _bdot(scores, v) + _bdot(q * jnp.exp(cumh), st, _NT)
        sa_scr[h] = st * jnp.exp(last) + _bdot(v, k * jnp.exp(last - cumh), _TN)
        on = o * lax.rsqrt(jnp.mean(o * o, axis=-1, keepdims=True) + EPS) * ghg_ref[:, cols]
        og = z_scr[:, o_og + h * HA_DV:o_og + (h + 1) * HA_DV]
        oa_scr[:, cols] = (on * _silu(og)).astype(BF16)
        yield

    cosf = cos_ref[...]
    sins = sin_ref[...]
    pos = lax.broadcasted_iota(jnp.int32, (t, RB_DK), 0).astype(F32)
    for h in range(RB_HEADS):
        log_gamma = math.log(1.0 - 2.0 ** (-5.0 - h))
        qb = z_scr[:, o_qb + h * RB_DK:o_qb + (h + 1) * RB_DK]
        kb = z_scr[:, o_kb + h * RB_DK:o_kb + (h + 1) * RB_DK]
        qr = (qb * cosf + pltpu.roll(qb, RB_DK // 2, 1) * sins) * (RB_DK ** -0.5)
        kr = kb * cosf + pltpu.roll(kb, RB_DK // 2, 1) * sins
        vcols = slice(h * RB_DV, (h + 1) * RB_DV)
        vb = z_scr[:, o_vb + h * RB_DV:o_vb + (h + 1) * RB_DV].astype(BF16)
        s = _bdot(qr, kr, _NT) * retw_scr[h]
        sb = jnp.where(first, 0.0, sb_scr[h])
        q_dec = jnp.exp(log_gamma * (pos + 1.0))
        k_dec = jnp.exp(log_gamma * (float(t - 1) - pos))
        o = _bdot(s, vb) + _bdot(qr * q_dec, sb)
        sb_scr[h] = math.exp(log_gamma * t) * sb + _bdot(kr * k_dec, vb, _TN)
        mu = jnp.mean(o, axis=-1, keepdims=True)
        oc = o - mu
        var = jnp.mean(oc * oc, axis=-1, keepdims=True)
        on = oc * lax.rsqrt(var + EPS) * gret_ref[:, vcols]
        og = z_scr[:, o_ogb + h * RB_DV:o_ogb + (h + 1) * RB_DV]
        ob_scr[:, vcols] = (on * _silu(og)).astype(BF16)
        yield

    d = x.shape[1]
    ya = jnp.dot(oa_scr[...], wa_ref[...], preferred_element_type=F32)
    yb = jnp.dot(ob_scr[...], wb_ref[...], preferred_element_type=F32)
    y = _sigmoid(z_scr[:, o_ma:o_ma + d]) * ya + _sigmoid(z_scr[:, o_mb:o_mb + d]) * yb
    o_ref[0] = x + gt1 * _bdot(y, wo_ref[...])


def _mixer_kernel(xp_ref, xm_ref, modp_ref, modm_ref, g1_ref, win_ref, lbl_ref, ghg_ref,
                  gret_ref, wa_ref, wb_ref, wo_ref, cos_ref, sin_ref, o_ref,
                  z0_scr, z1_scr, sa_scr, sb_scr, cum_scr, k_scr, retw_scr, lvl_scr, tri_scr,
                  oa_scr, ob_scr, *, layer, depth, tiles_per_seq):
    t = xp_ref.shape[1]
    n_levels = t.bit_length() - 1
    g = pl.program_id(0)

    @pl.when(g == 0)
    def _build_constants():
        row = lax.broadcasted_iota(jnp.int32, (t, t), 0)
        col = lax.broadcasted_iota(jnp.int32, (t, t), 1)
        diff = row ^ col
        lvl = jnp.zeros((t, t), jnp.int32)
        for bit in range(n_levels):
            lvl = lvl + (diff >= (1 << bit)).astype(jnp.int32)
        lvl_scr[...] = jnp.where(col > row, -1, lvl)
        tri_scr[...] = (col <= row).astype(BF16)
        dist = jnp.abs(row - col).astype(F32)
        visible = (col // CHUNK) <= (row // CHUNK)
        for h in range(RB_HEADS):
            log_gamma = math.log(1.0 - 2.0 ** (-5.0 - h))
            retw_scr[h] = jnp.where(visible, jnp.exp(log_gamma * dist), 0.0)
        z1_scr[...] = jnp.zeros_like(z1_scr)
        sa_scr[...] = jnp.zeros_like(sa_scr)
        sb_scr[...] = jnp.zeros_like(sb_scr)

    first = lax.rem(jnp.maximum(g - 1, 0), tiles_per_seq) == 0

    def step(z_write, z_read):
        proj = _project(xp_ref, modp_ref, g1_ref, win_ref, z_write)
        mix = _mix(z_read, xm_ref, modm_ref, first, lbl_ref, ghg_ref, gret_ref, wa_ref, wb_ref,
                   wo_ref, cos_ref, sin_ref, o_ref, sa_scr, sb_scr, cum_scr, k_scr, retw_scr,
                   lvl_scr, tri_scr, oa_scr, ob_scr, layer=layer, depth=depth)
        live = [proj, mix]
        while live:
            for gen in list(live):
                if next(gen, StopIteration) is StopIteration:
                    live.remove(gen)

    @pl.when(lax.rem(g, 2) == 0)
    def _even():
        step(z0_scr, z1_scr)

    @pl.when(lax.rem(g, 2) == 1)
    def _odd():
        step(z1_scr, z0_scr)


def _const_spec(shape):
    return pl.BlockSpec(shape, lambda g: (0,) * len(shape), pipeline_mode=pl.Buffered(1))


def _mixer(x, mod, g1, w_in, lb_logits, g_hgrn, g_ret, w_a, w_b, w_o, cos_t, sin_t, *, layer):
    bsz, seq, d = x.shape
    depth = mod.shape[0]
    t = MIX_TILE
    n_in = w_in.shape[1]
    tps = seq // t
    n_tiles = bsz * tps
    kern = functools.partial(_mixer_kernel, layer=layer, depth=depth, tiles_per_seq=tps)

    def proj_tile(g):
        return jnp.minimum(g, n_tiles - 1)

    def mix_tile(g):
        return jnp.maximum(g - 1, 0)

    return pl.pallas_call(
        kern,
        grid=(n_tiles + 1,),
        in_specs=[
            pl.BlockSpec((1, t, d), lambda g: (proj_tile(g) // tps, proj_tile(g) % tps, 0)),
            pl.BlockSpec((1, t, d), lambda g: (mix_tile(g) // tps, mix_tile(g) % tps, 0)),
            pl.BlockSpec((1, 1, 6, d), lambda g: (layer, proj_tile(g) // tps, 0, 0)),
            pl.BlockSpec((1, 1, 6, d), lambda g: (layer, mix_tile(g) // tps, 0, 0)),
            _const_spec((1, d)),
            _const_spec((d, n_in)),
            _const_spec((depth, HA_K)),
            _const_spec((1, HA_V)),
            _const_spec((1, RB_V)),
            _const_spec((HA_V, d)),
            _const_spec((RB_V, d)),
            _const_spec((d, d)),
            pl.BlockSpec((t, RB_DK), lambda g: (mix_tile(g) % tps, 0)),
            pl.BlockSpec((t, RB_DK), lambda g: (mix_tile(g) % tps, 0)),
        ],
        out_specs=pl.BlockSpec((1, t, d), lambda g: (mix_tile(g) // tps, mix_tile(g) % tps, 0)),
        out_shape=jax.ShapeDtypeStruct((bsz, seq, d), F32),
        scratch_shapes=[
            pltpu.VMEM((t, n_in), F32),
            pltpu.VMEM((t, n_in), F32),
            pltpu.VMEM((HA_HEADS, HA_DV, HA_DK), F32),
            pltpu.VMEM((RB_HEADS, RB_DK, RB_DV), F32),
            pltpu.VMEM((t, HA_K), F32),
            pltpu.VMEM((t, HA_K), F32),
            pltpu.VMEM((RB_HEADS, t, t), F32),
            pltpu.VMEM((t, t), jnp.int32),
            pltpu.VMEM((t, t), BF16),
            pltpu.VMEM((t, HA_V), BF16),
            pltpu.VMEM((t, RB_V), BF16),
        ],
        compiler_params=pltpu.CompilerParams(
            dimension_semantics=("arbitrary",), vmem_limit_bytes=VMEM_LIMIT_BYTES),
        name=f"mixer_layer{layer}",
    )(x, x, mod, mod, g1, w_in, lb_logits, g_hgrn, g_ret, w_a, w_b, w_o, cos_t, sin_t)


def _lane_shift(a, delta):
    return pltpu.roll(a, (-delta) % a.shape[1], 1)


def _route(scores, biased):
    lane = lax.broadcasted_iota(jnp.int32, scores.shape, 1)
    pos = lane & (EXP_PER_GROUP - 1)
    grp = lane // EXP_PER_GROUP
    rank = jnp.zeros(scores.shape, jnp.int32)
    for delta in range(-(EXP_PER_GROUP - 1), EXP_PER_GROUP):
        if delta == 0:
            continue
        other = _lane_shift(biased, delta)
        ahead = (other > biased) | ((other == biased) & (delta < 0))
        in_grp = (pos + delta >= 0) & (pos + delta < EXP_PER_GROUP)
        rank = rank + (ahead & in_grp).astype(jnp.int32)
    top = rank < TOP_K
    kept = jnp.where(top, biased, 0.0)
    grp_score = kept
    for delta in range(-(EXP_PER_GROUP - 1), EXP_PER_GROUP):
        if delta == 0:
            continue
        in_grp = (pos + delta >= 0) & (pos + delta < EXP_PER_GROUP)
        grp_score = grp_score + jnp.where(in_grp, _lane_shift(kept, delta), 0.0)
    grank = jnp.zeros(scores.shape, jnp.int32)
    for dg in range(-(N_GROUPS - 1), N_GROUPS):
        if dg == 0:
            continue
        other = _lane_shift(grp_score, dg * EXP_PER_GROUP)
        ahead = (other > grp_score) | ((other == grp_score) & (dg < 0))
        in_rng = (grp + dg >= 0) & (grp + dg < N_GROUPS)
        grank = grank + (ahead & in_rng).astype(jnp.int32)
    sel = top & (grank == 0) & (lane < N_EXPERTS)
    w = jnp.where(sel, scores, 0.0)
    return w / jnp.sum(w, axis=-1, keepdims=True), sel


def _router_kernel(x_ref, mod_ref, g2_ref, wr_ref, br_ref, p_ref, idx_ref, cnt_ref,
                   before_scr, run_scr):
    tm, d = x_ref.shape
    half = d // 2
    i = pl.program_id(0)

    @pl.when(i == 0)
    def _init():
        row = lax.broadcasted_iota(jnp.int32, (tm, tm), 0)
        col = lax.broadcasted_iota(jnp.int32, (tm, tm), 1)
        before_scr[...] = (row < col).astype(BF16)
        run_scr[...] = jnp.zeros_like(run_scr)

    x = x_ref[...]
    modv = mod_ref[0, 0]
    sh2, sc2 = modv[3:4], modv[4:5]
    ms = jnp.mean(x * x, axis=-1, keepdims=True)
    h = (x * lax.rsqrt(ms + EPS)) * g2_ref[...] * (1.0 + sc2) + sh2
    h_hi = h.astype(BF16)
    h_hi32 = h_hi.astype(F32)
    h_lo = (h - h_hi32).astype(BF16)
    wr = wr_ref[...]
    w_hi = wr.astype(BF16)
    w_lo = (wr - w_hi.astype(F32)).astype(BF16)
    logits = (jnp.dot(h_hi, w_hi, preferred_element_type=F32)
              + jnp.dot(h_lo, w_hi, preferred_element_type=F32)
              + jnp.dot(h_hi, w_lo, preferred_element_type=F32))
    scores = _sigmoid(logits)
    comb, sel = _route(scores, scores + br_ref[...])

    bits = pltpu.bitcast(h_hi32, jnp.uint32)
    p_ref[:, 0:half] = (bits[:, 0:half] >> 16) | bits[:, half:d]
    p_ref[:, half:half + LANES] = pltpu.bitcast(comb, jnp.uint32)

    sel_t = sel.astype(F32).T
    rows = [sel_t[e:e + 1, :] for e in range(N_EXPERTS)]
    bucket = jnp.zeros((1, tm), F32)
    for g in range(N_GROUPS):
        grp_rows = rows[g * EXP_PER_GROUP:(g + 1) * EXP_PER_GROUP]
        bucket = bucket + float(g * N_PAIRS) * functools.reduce(jnp.maximum, grp_rows)
        for p, (a, b) in enumerate(PAIRS):
            if p:
                bucket = bucket + float(p) * (grp_rows[a] * grp_rows[b])
    bucket = bucket.astype(jnp.int32)
    onehot = (lax.broadcasted_iota(jnp.int32, (BUCKET_ROWS, tm), 0) == bucket).astype(F32)
    before = jnp.dot(onehot.astype(BF16), before_scr[...], preferred_element_type=F32)
    run = run_scr[...]
    rank = jnp.sum(onehot * (before + run[:, 0:1]), axis=0, keepdims=True)
    run = run + jnp.sum(onehot, axis=1, keepdims=True)
    run_scr[...] = run
    idx_ref[0] = jnp.concatenate(
        [bucket, rank.astype(jnp.int32), jnp.zeros((SUBLANES - 2, tm), jnp.int32)], axis=0)
    cnt_ref[...] = run.astype(jnp.int32)


def _router(x2d, mod, g2, wr_pad, br_pad, *, layer, seq):
    n_tok, d = x2d.shape
    tm = ROUTE_TILE
    n_tiles = n_tok // tm
    row_w = d // 2 + LANES
    return pl.pallas_call(
        _router_kernel,
        grid=(n_tiles,),
        in_specs=[
            pl.BlockSpec((tm, d), lambda i: (i, 0)),
            pl.BlockSpec((1, 1, 6, d), lambda i: (layer, (i * tm) // seq, 0, 0)),
            pl.BlockSpec((1, d), lambda i: (0, 0)),
            pl.BlockSpec((d, LANES), lambda i: (0, 0)),
            pl.BlockSpec((1, LANES), lambda i: (0, 0)),
        ],
        out_specs=[
            pl.BlockSpec((tm, row_w), lambda i: (i, 0)),
            pl.BlockSpec((1, SUBLANES, tm), lambda i: (i, 0, 0)),
            pl.BlockSpec((BUCKET_ROWS, LANES), lambda i: (0, 0)),
        ],
        out_shape=[
            jax.ShapeDtypeStruct((n_tok, row_w), jnp.uint32),
            jax.ShapeDtypeStruct((n_tiles, SUBLANES, tm), jnp.int32),
            jax.ShapeDtypeStruct((BUCKET_ROWS, LANES), jnp.int32),
        ],
        scratch_shapes=[
            pltpu.VMEM((tm, tm), BF16),
            pltpu.VMEM((BUCKET_ROWS, LANES), F32),
        ],
        compiler_params=pltpu.CompilerParams(
            dimension_semantics=("arbitrary",), vmem_limit_bytes=VMEM_LIMIT_BYTES),
        name=f"moe_router{layer}",
    )(x2d, mod, g2, wr_pad, br_pad)


def _dispatch_kernel(dest_ref, p_ref, init_ref, hs_ref, sem):
    del init_ref
    td = p_ref.shape[0]

    def issue(i, carry):
        for prio in range(DMA_PRIORITIES):
            r = i * DMA_PRIORITIES + prio
            pltpu.make_async_copy(p_ref.at[pl.ds(r, 1), :],
                                  hs_ref.at[pl.ds(dest_ref[0, 0, r], 1), :],
                                  sem).start(priority=prio)
        return carry

    lax.fori_loop(0, td // DMA_PRIORITIES, issue, 0, unroll=4)

    def drain(r, carry):
        pltpu.make_async_copy(p_ref.at[pl.ds(0, 1), :], hs_ref.at[pl.ds(0, 1), :], sem).wait()
        return carry

    lax.fori_loop(0, td, drain, 0, unroll=8)


def _dispatch(p, dest2d, n_sorted, *, layer):
    n_tok, row_w = p.shape
    td = dest2d.shape[2]
    init = jnp.zeros((n_sorted, row_w), jnp.uint32)
    return pl.pallas_call(
        _dispatch_kernel,
        grid=(n_tok // td,),
        in_specs=[
            pl.BlockSpec((1, 1, td), lambda i: (i, 0, 0), memory_space=pltpu.SMEM),
            pl.BlockSpec((td, row_w), lambda i: (i, 0)),
            pl.BlockSpec(memory_space=pl.ANY),
        ],
        out_specs=pl.BlockSpec(memory_space=pl.ANY),
        out_shape=jax.ShapeDtypeStruct((n_sorted, row_w), jnp.uint32),
        scratch_shapes=[pltpu.SemaphoreType.DMA(())],
        input_output_aliases={2: 0},
        compiler_params=pltpu.CompilerParams(
            dimension_semantics=("arbitrary",), vmem_limit_bytes=VMEM_LIMIT_BYTES),
        name=f"moe_dispatch{layer}",
    )(dest2d, p, init)


def _expert_kernel(e1_ref, e2_ref, valid_ref, hs_ref, wg1_ref, wu1_ref, wd1_ref,
                   wg2_ref, wu2_ref, wd2_ref, y_ref):
    j = pl.program_id(0)
    half = hs_ref.shape[1] - LANES

    @pl.when(valid_ref[j] != 0)
    def _compute():
        packed = hs_ref[:, 0:half]
        lo = pltpu.bitcast(packed << 16, F32)
        hi = pltpu.bitcast(packed & jnp.uint32(0xFFFF0000), F32)
        h = jnp.concatenate([lo, hi], axis=1).astype(BF16)
        comb = pltpu.bitcast(hs_ref[:, half:half + LANES], F32)
        lane = lax.broadcasted_iota(jnp.int32, comb.shape, 1)

        def expert(e, wg_ref, wu_ref, wd_ref):
            ce = jnp.sum(jnp.where(lane == e, comb, 0.0), axis=-1, keepdims=True)
            hg = jnp.dot(h, wg_ref[0], preferred_element_type=F32)
            hu = jnp.dot(h, wu_ref[0], preferred_element_type=F32)
            return ce * _bdot(_silu(hg) * hu, wd_ref[0])

        y_ref[...] = (expert(e1_ref[j], wg1_ref, wu1_ref, wd1_ref)
                      + expert(e2_ref[j], wg2_ref, wu2_ref, wd2_ref))

    @pl.when(valid_ref[j] == 0)
    def _empty():
        y_ref[...] = jnp.zeros_like(y_ref)


def _experts(hs, e1, e2, valid, w_gate, w_up, w_down, *, layer):
    n_sorted, row_w = hs.shape
    _, d, d_exp = w_gate.shape
    ts = SORT_TILE
    first = lambda j, e1, e2, valid: (e1[j], 0, 0)
    second = lambda j, e1, e2, valid: (e2[j], 0, 0)
    grid_spec = pltpu.PrefetchScalarGridSpec(
        num_scalar_prefetch=3,
        grid=(n_sorted // ts,),
        in_specs=[
            pl.BlockSpec((ts, row_w), lambda j, e1, e2, valid: (j, 0)),
            pl.BlockSpec((1, d, d_exp), first),
            pl.BlockSpec((1, d, d_exp), first),
            pl.BlockSpec((1, d_exp, d), first),
            pl.BlockSpec((1, d, d_exp), second),
            pl.BlockSpec((1, d, d_exp), second),
            pl.BlockSpec((1, d_exp, d), second),
        ],
        out_specs=pl.BlockSpec((ts, d), lambda j, e1, e2, valid: (j, 0)),
    )
    return pl.pallas_call(
        _expert_kernel,
        grid_spec=grid_spec,
        out_shape=jax.ShapeDtypeStruct((n_sorted, d), F32),
        compiler_params=pltpu.CompilerParams(
            dimension_semantics=("arbitrary",), vmem_limit_bytes=VMEM_LIMIT_BYTES),
        name=f"moe_experts{layer}",
    )(e1, e2, valid, hs, w_gate, w_up, w_down, w_gate, w_up, w_down)


def _combine_kernel(dest_ref, x_ref, mod_ref, gf_ref, ys_ref, o_ref, y_scr, sem, *, final_norm):
    tc = x_ref.shape[0]

    def issue(i, carry):
        for prio in range(DMA_PRIORITIES):
            r = i * DMA_PRIORITIES + prio
            pltpu.make_async_copy(ys_ref.at[pl.ds(dest_ref[0, 0, r], 1), :],
                                  y_scr.at[pl.ds(r, 1), :], sem).start(priority=prio)
        return carry

    lax.fori_loop(0, tc // DMA_PRIORITIES, issue, 0, unroll=4)

    def drain(r, carry):
        pltpu.make_async_copy(ys_ref.at[pl.ds(0, 1), :], y_scr.at[pl.ds(0, 1), :], sem).wait()
        return carry

    lax.fori_loop(0, tc, drain, 0, unroll=8)

    gt2 = mod_ref[0, 0][5:6]
    out = x_ref[...] + gt2 * y_scr[...]
    if final_norm:
        ms = jnp.mean(out * out, axis=-1, keepdims=True)
        out = out * lax.rsqrt(ms + EPS) * gf_ref[...]
    o_ref[...] = out


def _combine(x2d, mod, g_final, ys, dest2d, *, layer, seq, final_norm):
    n_tok, d = x2d.shape
    tc = dest2d.shape[2]
    kern = functools.partial(_combine_kernel, final_norm=final_norm)
    return pl.pallas_call(
        kern,
        grid=(n_tok // tc,),
        in_specs=[
            pl.BlockSpec((1, 1, tc), lambda i: (i, 0, 0), memory_space=pltpu.SMEM),
            pl.BlockSpec((tc, d), lambda i: (i, 0)),
            pl.BlockSpec((1, 1, 6, d), lambda i: (layer, (i * tc) // seq, 0, 0)),
            pl.BlockSpec((1, d), lambda i: (0, 0)),
            pl.BlockSpec(memory_space=pl.ANY),
        ],
        out_specs=pl.BlockSpec((tc, d), lambda i: (i, 0)),
        out_shape=jax.ShapeDtypeStruct((n_tok, d), F32),
        scratch_shapes=[pltpu.VMEM((tc, d), F32), pltpu.SemaphoreType.DMA(())],
        compiler_params=pltpu.CompilerParams(
            dimension_semantics=("arbitrary",), vmem_limit_bytes=VMEM_LIMIT_BYTES),
        name=f"moe_combine{layer}",
    )(dest2d, x2d, mod, g_final, ys)


def _moe(x2d, mod, g2, wr_pad, br_pad, w_gate, w_up, w_down, g_final, *, layer, seq, final_norm):
    n_tok, _ = x2d.shape
    p, idx, cnt = _router(x2d, mod, g2, wr_pad, br_pad, layer=layer, seq=seq)

    n_buckets = N_GROUPS * N_PAIRS
    counts = cnt[:n_buckets, 0]
    padded = ((counts + SORT_TILE - 1) // SORT_TILE) * SORT_TILE
    ends = jnp.cumsum(padded)
    starts = ends - padded
    dest = starts[idx[:, 0, :].reshape(n_tok)] + idx[:, 1, :].reshape(n_tok)
    n_tiles = n_tok // SORT_TILE + n_buckets
    tile_start = jnp.arange(n_tiles, dtype=jnp.int32) * SORT_TILE
    tile_bucket = jnp.sum((ends[None, :] <= tile_start[:, None]).astype(jnp.int32), axis=1)
    valid = (tile_bucket < n_buckets).astype(jnp.int32)
    tb = jnp.minimum(tile_bucket, n_buckets - 1)
    grp, pair = tb // N_PAIRS, tb % N_PAIRS
    pair_lo = jnp.asarray([a for a, _ in PAIRS], jnp.int32)
    pair_hi = jnp.asarray([b for _, b in PAIRS], jnp.int32)
    e1 = grp * EXP_PER_GROUP + pair_lo[pair]
    e2 = grp * EXP_PER_GROUP + pair_hi[pair]

    hs = _dispatch(p, dest.reshape(n_tok // DISPATCH_TILE, 1, DISPATCH_TILE),
                   n_tiles * SORT_TILE, layer=layer)
    ys = _experts(hs, e1, e2, valid, w_gate, w_up, w_down, layer=layer)
    return _combine(x2d, mod, g_final, ys, dest.reshape(n_tok // COMBINE_TILE, 1, COMBINE_TILE),
                    layer=layer, seq=seq, final_norm=final_norm)


def _rope_tables(seq):
    half = RB_DK // 2
    inv = ROPE_BASE ** (-jnp.arange(half, dtype=F32) / half)
    ang = jnp.arange(seq, dtype=F32)[:, None] * inv[None, :]
    cos, sin = jnp.cos(ang), jnp.sin(ang)
    return jnp.concatenate([cos, cos], axis=-1), jnp.concatenate([-sin, sin], axis=-1)


def kernel(x, c, w_ada, b_ada, g_norm1, g_norm2, w_in, lb_logits, g_hgrn, g_ret, w_branch_a,
           w_branch_b, w_out, w_router, b_router, w_exp_gate, w_exp_up, w_exp_down, g_final):
    bsz, seq, d = x.shape
    depth = w_ada.shape[0]
    assert seq % MIX_TILE == 0 and MIX_TILE % CHUNK == 0
    assert seq % ROUTE_TILE == 0 and seq % COMBINE_TILE == 0
    assert (bsz * seq) % DISPATCH_TILE == 0 and (bsz * seq) % SORT_TILE == 0
    assert w_ada.shape[2] == 6 * d and w_ada.shape[2] % MOD_COLS == 0
    assert N_GROUPS * N_PAIRS <= BUCKET_ROWS and TOP_K == 2

    mod = _modulation(c, w_ada, b_ada).reshape(depth, bsz, 6, d)
    cos_t, sin_t = _rope_tables(seq)
    wr_pad = jnp.pad(w_router, ((0, 0), (0, LANES - N_EXPERTS)))
    br_pad = jnp.pad(b_router.astype(F32), (0, LANES - N_EXPERTS)).reshape(1, LANES)
    lbl = lb_logits.astype(F32)

    for l in range(depth):
        x = _mixer(x, mod, g_norm1[l].reshape(1, d), w_in[l].astype(BF16), lbl,
                   g_hgrn[l].reshape(1, HA_V), g_ret[l].reshape(1, RB_V),
                   w_branch_a[l].astype(BF16), w_branch_b[l].astype(BF16),
                   w_out[l].astype(BF16), cos_t, sin_t, layer=l)
        x = _moe(x.reshape(bsz * seq, d), mod, g_norm2[l].reshape(1, d), wr_pad, br_pad,
                 w_exp_gate[l].astype(BF16), w_exp_up[l].astype(BF16),
                 w_exp_down[l].astype(BF16), g_final.reshape(1, d),
                 layer=l, seq=seq, final_norm=(l == depth - 1)).reshape(bsz, seq, d)
    return x
```

```python
import functools
import math

import jax
import jax.numpy as jnp
from jax import lax
from jax.experimental import pallas as pl
from jax.experimental.pallas import tpu as pltpu

F32 = jnp.float32
BF16 = jnp.bfloat16

EPS = 1e-6
CHUNK = 64
ROPE_BASE = 10000.0
HA_HEADS, HA_DK, HA_DV = 4, 128, 128
RB_HEADS, RB_DK, RB_DV = 4, 128, 256
HA_K, HA_V = HA_HEADS * HA_DK, HA_HEADS * HA_DV
RB_K, RB_V = RB_HEADS * RB_DK, RB_HEADS * RB_DV
N_EXPERTS, N_GROUPS, TOP_K = 16, 4, 2
EXP_PER_GROUP = N_EXPERTS // N_GROUPS

LANES = 128
SUBLANES = 8
VMEM_LIMIT_BYTES = 56 * 1024 * 1024

MIX_TILE = 256
FAST_BLOCK = 32
SAFE_PEAK = 1e37
ROUTE_TILE = 1024
DISPATCH_TILE = 2048
SORT_TILE = 512
COMBINE_TILE = 1024
MOD_COLS = 1536
DMA_PRIORITIES = 2

PAIRS = tuple((a, b) for a in range(EXP_PER_GROUP) for b in range(a + 1, EXP_PER_GROUP))
N_PAIRS = len(PAIRS)
BUCKET_ROWS = 32

_NT = (((1,), (1,)), ((), ()))
_TN = (((0,), (0,)), ((), ()))


def _sigmoid(v):
    return jax.nn.sigmoid(v)


def _silu(v):
    return v * jax.nn.sigmoid(v)


def _bdot(a, b, dims=None):
    a = a.astype(BF16)
    b = b.astype(BF16)
    if dims is None:
        return jnp.dot(a, b, preferred_element_type=F32)
    return lax.dot_general(a, b, dims, preferred_element_type=F32)


def _mod_kernel(c_ref, w_ref, b_ref, o_ref):
    o_ref[0] = _bdot(_silu(c_ref[...]), w_ref[0]) + b_ref[0]


def _modulation(c, w_ada, b_ada):
    depth, d, six_d = w_ada.shape
    bsz = c.shape[0]
    return pl.pallas_call(
        _mod_kernel,
        grid=(depth, six_d // MOD_COLS),
        in_specs=[
            pl.BlockSpec((bsz, d), lambda l, j: (0, 0)),
            pl.BlockSpec((1, d, MOD_COLS), lambda l, j: (l, 0, j)),
            pl.BlockSpec((1, 1, MOD_COLS), lambda l, j: (l, 0, j)),
        ],
        out_specs=pl.BlockSpec((1, bsz, MOD_COLS), lambda l, j: (l, 0, j)),
        out_shape=jax.ShapeDtypeStruct((depth, bsz, six_d), F32),
        compiler_params=pltpu.CompilerParams(
            dimension_semantics=("arbitrary", "arbitrary"),
            vmem_limit_bytes=VMEM_LIMIT_BYTES),
        name="adaln_modulation",
    )(c, w_ada, b_ada.reshape(depth, 1, six_d))


def _level_ref(cum_scr, cumh, cols, block):
    t = cumh.shape[0]
    half = block // 2
    if block >= SUBLANES:
        pieces = []
        for m in range(t // block):
            r = m * block + half - 1
            pieces.append(jnp.broadcast_to(cum_scr[r:r + 1, cols], (block, cumh.shape[1])))
        return pieces[0] if len(pieces) == 1 else jnp.concatenate(pieces, axis=0)
    pos = lax.broadcasted_iota(jnp.int32, cumh.shape, 0) & (block - 1)
    ref = cumh
    for off in range(-half, half):
        if off == 0:
            continue
        shifted = pltpu.roll(cumh, (-off) % t, 0)
        ref = jnp.where(pos == half - 1 - off, shifted, ref)
    return ref


def _project(x_ref, mod_ref, g1_ref, win_ref, z_scr):
    x = x_ref[0]
    modv = mod_ref[0, 0]
    sh1, sc1 = modv[0:1], modv[1:2]
    ms = jnp.mean(x * x, axis=-1, keepdims=True)
    hn = (x * lax.rsqrt(ms + EPS)) * g1_ref[...]
    hb = (hn * (1.0 + sc1) + sh1).astype(BF16)
    yield
    n_in = win_ref.shape[1]
    seg = 1024
    for s in range(n_in // seg):
        z_scr[:, s * seg:(s + 1) * seg] = jnp.dot(
            hb, win_ref[:, s * seg:(s + 1) * seg], preferred_element_type=F32)
        yield


def _mix(z_scr, x_ref, mod_ref, first, lbl_ref, ghg_ref, gret_ref, wa_ref, wb_ref, wo_ref,
         cos_ref, sin_ref, o_ref, flag_ref, sa_scr, sb_scr, cum_scr, k_scr, retw_scr, lvl_scr,
         tri_scr, oa_scr, ob_scr, *, layer, depth, base_block):
    t = x_ref.shape[1]
    n_levels = t.bit_length() - 1
    x = x_ref[0]
    gt1 = mod_ref[0, 0][2:3]

    rows = [lbl_ref[i:i + 1, :] for i in range(depth)]
    mx = functools.reduce(jnp.maximum, rows)
    ex = [jnp.exp(r - mx) for r in rows]
    tot = functools.reduce(lambda a, b: a + b, ex)
    lb = functools.reduce(lambda a, b: a + b, [e / tot for e in ex[:layer + 1]]) - ex[0] / tot

    o_q, o_f, o_i, o_og = 0, HA_K, 2 * HA_K, 2 * HA_K + HA_V
    o_qb = o_og + HA_V
    o_kb = o_qb + RB_K
    o_vb = o_kb + RB_K
    o_ogb = o_vb + RB_V
    o_ma = o_ogb + RB_V
    o_mb = o_ma + x.shape[1]

    f = lb + (1.0 - lb) * _sigmoid(z_scr[:, o_f:o_f + HA_K])
    k_scr[...] = 1.0 - f
    g = jnp.log(f)
    g_hi = g.astype(BF16)
    g_lo = (g - g_hi.astype(F32)).astype(BF16)
    tri = tri_scr[...]
    cum_scr[...] = (jnp.dot(tri, g_hi, preferred_element_type=F32)
                    + jnp.dot(tri, g_lo, preferred_element_type=F32))

    yield

    lvl = lvl_scr[...]
    base_level = base_block.bit_length() - 1
    peak = jnp.zeros((1, HA_DK), F32)
    for h in range(HA_HEADS):
        cols = slice(h * HA_DK, (h + 1) * HA_DK)
        q = _silu(z_scr[:, o_q + h * HA_DK:o_q + (h + 1) * HA_DK])
        k = k_scr[:, cols]
        v = z_scr[:, o_i + h * HA_DV:o_i + (h + 1) * HA_DV].astype(BF16)
        cumh = cum_scr[:, cols]
        scores = jnp.zeros((t, t), F32)
        for level in range(base_level, n_levels + 1):
            if level > base_level:
                ref = _level_ref(cum_scr, cumh, cols, 1 << level)
                e = jnp.exp(-jnp.abs(cumh - ref))
                qs, ks = q * e, k * e
                mask = lvl == level
            elif base_block == 1:
                qs, ks = q, k
                mask = lvl == 0
            else:
                shift = cumh - _level_ref(cum_scr, cumh, cols, base_block)
                qs, ks = q * jnp.exp(shift), k * jnp.exp(-shift)
                big = jnp.maximum(jnp.abs(qs), jnp.abs(ks))
                peak = jnp.maximum(peak, jnp.max(big, axis=0, keepdims=True))
                mask = (lvl >= 0) & (lvl <= base_level)
            scores = jnp.where(mask, _bdot(qs, ks, _NT), scores)
        st = jnp.where(first, 0.0, sa_scr[h])
        last = cum_scr[t - 1:t, cols]
        o = _bdot(scores, v) + _bdot(q * jnp.exp(cumh), st, _NT)
        sa_scr[h] = st * jnp.exp(last) + _bdot(v, k * jnp.exp(last - cumh), _TN)
        on = o * lax.rsqrt(jnp.mean(o * o, axis=-1, keepdims=True) + EPS) * ghg_ref[:, cols]
        og = z_scr[:, o_og + h * HA_DV:o_og + (h + 1) * HA_DV]
        oa_scr[:, cols] = (on * _silu(og)).astype(BF16)
        yield

    flag_ref[0] = jnp.broadcast_to(peak, (SUBLANES, HA_DK))

    cosf = cos_ref[...]
    sins = sin_ref[...]
    pos = lax.broadcasted_iota(jnp.int32, (t, RB_DK), 0).astype(F32)
    for h in range(RB_HEADS):
        log_gamma = math.log(1.0 - 2.0 ** (-5.0 - h))
        qb = z_scr[:, o_qb + h * RB_DK:o_qb + (h + 1) * RB_DK]
        kb = z_scr[:, o_kb + h * RB_DK:o_kb + (h + 1) * RB_DK]
        qr = (qb * cosf + pltpu.roll(qb, RB_DK // 2, 1) * sins) * (RB_DK ** -0.5)
        kr = kb * cosf + pltpu.roll(kb, RB_DK // 2, 1) * sins
        vcols = slice(h * RB_DV, (h + 1) * RB_DV)
        vb = z_scr[:, o_vb + h * RB_DV:o_vb + (h + 1) * RB_DV].astype(BF16)
        s = _bdot(qr, kr, _NT) * retw_scr[h]
        sb = jnp.where(first, 0.0, sb_scr[h])
        q_dec = jnp.exp(log_gamma * (pos + 1.0))
        k_dec = jnp.exp(log_gamma * (float(t - 1) - pos))
        o = _bdot(s, vb) + _bdot(qr * q_dec, sb)
        sb_scr[h] = math.exp(log_gamma * t) * sb + _bdot(kr * k_dec, vb, _TN)
        mu = jnp.mean(o, axis=-1, keepdims=True)
        oc = o - mu
        var = jnp.mean(oc * oc, axis=-1, keepdims=True)
        on = oc * lax.rsqrt(var + EPS) * gret_ref[:, vcols]
        og = z_scr[:, o_ogb + h * RB_DV:o_ogb + (h + 1) * RB_DV]
        ob_scr[:, vcols] = (on * _silu(og)).astype(BF16)
        yield

    d = x.shape[1]
    ya = jnp.dot(oa_scr[...], wa_ref[...], preferred_element_type=F32)
    yb = jnp.dot(ob_scr[...], wb_ref[...], preferred_element_type=F32)
    y = _sigmoid(z_scr[:, o_ma:o_ma + d]) * ya + _sigmoid(z_scr[:, o_mb:o_mb + d]) * yb
    o_ref[0] = x + gt1 * _bdot(y, wo_ref[...])


def _mixer_kernel(xp_ref, xm_ref, modp_ref, modm_ref, g1_ref, win_ref, lbl_ref, ghg_ref,
                  gret_ref, wa_ref, wb_ref, wo_ref, cos_ref, sin_ref, o_ref, flag_ref,
                  z0_scr, z1_scr, sa_scr, sb_scr, cum_scr, k_scr, retw_scr, lvl_scr, tri_scr,
                  oa_scr, ob_scr, *, layer, depth, tiles_per_seq, base_block):
    t = xp_ref.shape[1]
    n_levels = t.bit_length() - 1
    g = pl.program_id(0)

    @pl.when(g == 0)
    def _build_constants():
        row = lax.broadcasted_iota(jnp.int32, (t, t), 0)
        col = lax.broadcasted_iota(jnp.int32, (t, t), 1)
        diff = row ^ col
        lvl = jnp.zeros((t, t), jnp.int32)
        for bit in range(n_levels):
            lvl = lvl + (diff >= (1 << bit)).astype(jnp.int32)
        lvl_scr[...] = jnp.where(col > row, -1, lvl)
        tri_scr[...] = (col <= row).astype(BF16)
        dist = jnp.abs(row - col).astype(F32)
        visible = (col // CHUNK) <= (row // CHUNK)
        for h in range(RB_HEADS):
            log_gamma = math.log(1.0 - 2.0 ** (-5.0 - h))
            retw_scr[h] = jnp.where(visible, jnp.exp(log_gamma * dist), 0.0)
        z1_scr[...] = jnp.zeros_like(z1_scr)
        sa_scr[...] = jnp.zeros_like(sa_scr)
        sb_scr[...] = jnp.zeros_like(sb_scr)

    first = lax.rem(jnp.maximum(g - 1, 0), tiles_per_seq) == 0

    def step(z_write, z_read):
        proj = _project(xp_ref, modp_ref, g1_ref, win_ref, z_write)
        mix = _mix(z_read, xm_ref, modm_ref, first, lbl_ref, ghg_ref, gret_ref, wa_ref, wb_ref,
                   wo_ref, cos_ref, sin_ref, o_ref, flag_ref, sa_scr, sb_scr, cum_scr, k_scr,
                   retw_scr, lvl_scr, tri_scr, oa_scr, ob_scr, layer=layer, depth=depth,
                   base_block=base_block)
        live = [proj, mix]
        while live:
            for gen in list(live):
                if next(gen, StopIteration) is StopIteration:
                    live.remove(gen)

    @pl.when(lax.rem(g, 2) == 0)
    def _even():
        step(z0_scr, z1_scr)

    @pl.when(lax.rem(g, 2) == 1)
    def _odd():
        step(z1_scr, z0_scr)


def _const_spec(shape):
    return pl.BlockSpec(shape, lambda g: (0,) * len(shape), pipeline_mode=pl.Buffered(1))


def _mixer(x, mod, g1, w_in, lb_logits, g_hgrn, g_ret, w_a, w_b, w_o, cos_t, sin_t, *, layer,
           base_block):
    bsz, seq, d = x.shape
    depth = mod.shape[0]
    t = MIX_TILE
    n_in = w_in.shape[1]
    tps = seq // t
    n_tiles = bsz * tps
    kern = functools.partial(_mixer_kernel, layer=layer, depth=depth, tiles_per_seq=tps,
                             base_block=base_block)

    def proj_tile(g):
        return jnp.minimum(g, n_tiles - 1)

    def mix_tile(g):
        return jnp.maximum(g - 1, 0)

    return pl.pallas_call(
        kern,
        grid=(n_tiles + 1,),
        in_specs=[
            pl.BlockSpec((1, t, d), lambda g: (proj_tile(g) // tps, proj_tile(g) % tps, 0)),
            pl.BlockSpec((1, t, d), lambda g: (mix_tile(g) // tps, mix_tile(g) % tps, 0)),
            pl.BlockSpec((1, 1, 6, d), lambda g: (layer, proj_tile(g) // tps, 0, 0)),
            pl.BlockSpec((1, 1, 6, d), lambda g: (layer, mix_tile(g) // tps, 0, 0)),
            _const_spec((1, d)),
            _const_spec((d, n_in)),
            _const_spec((depth, HA_K)),
            _const_spec((1, HA_V)),
            _const_spec((1, RB_V)),
            _const_spec((HA_V, d)),
            _const_spec((RB_V, d)),
            _const_spec((d, d)),
            pl.BlockSpec((t, RB_DK), lambda g: (mix_tile(g) % tps, 0)),
            pl.BlockSpec((t, RB_DK), lambda g: (mix_tile(g) % tps, 0)),
        ],
        out_specs=[
            pl.BlockSpec((1, t, d), lambda g: (mix_tile(g) // tps, mix_tile(g) % tps, 0)),
            pl.BlockSpec((1, SUBLANES, HA_DK), lambda g: (g, 0, 0)),
        ],
        out_shape=[
            jax.ShapeDtypeStruct((bsz, seq, d), F32),
            jax.ShapeDtypeStruct((n_tiles + 1, SUBLANES, HA_DK), F32),
        ],
        scratch_shapes=[
            pltpu.VMEM((t, n_in), F32),
            pltpu.VMEM((t, n_in), F32),
            pltpu.VMEM((HA_HEADS, HA_DV, HA_DK), F32),
            pltpu.VMEM((RB_HEADS, RB_DK, RB_DV), F32),
            pltpu.VMEM((t, HA_K), F32),
            pltpu.VMEM((t, HA_K), F32),
            pltpu.VMEM((RB_HEADS, t, t), F32),
            pltpu.VMEM((t, t), jnp.int32),
            pltpu.VMEM((t, t), BF16),
            pltpu.VMEM((t, HA_V), BF16),
            pltpu.VMEM((t, RB_V), BF16),
        ],
        compiler_params=pltpu.CompilerParams(
            dimension_semantics=("arbitrary",), vmem_limit_bytes=VMEM_LIMIT_BYTES),
        name=f"mixer_layer{layer}_block{base_block}",
    )(x, x, mod, mod, g1, w_in, lb_logits, g_hgrn, g_ret, w_a, w_b, w_o, cos_t, sin_t)


def _mixer_layer(x, *args, layer):
    fast, peak = _mixer(x, *args, layer=layer, base_block=FAST_BLOCK)
    safe = jnp.max(peak) <= SAFE_PEAK
    return lax.cond(safe, lambda: fast,
                    lambda: _mixer(x, *args, layer=layer, base_block=1)[0])


def _lane_shift(a, delta):
    return pltpu.roll(a, (-delta) % a.shape[1], 1)


def _route(scores, biased):
    lane = lax.broadcasted_iota(jnp.int32, scores.shape, 1)
    pos = lane & (EXP_PER_GROUP - 1)
    grp = lane // EXP_PER_GROUP
    rank = jnp.zeros(scores.shape, jnp.int32)
    for delta in range(-(EXP_PER_GROUP - 1), EXP_PER_GROUP):
        if delta == 0:
            continue
        other = _lane_shift(biased, delta)
        ahead = (other > biased) | ((other == biased) & (delta < 0))
        in_grp = (pos + delta >= 0) & (pos + delta < EXP_PER_GROUP)
        rank = rank + (ahead & in_grp).astype(jnp.int32)
    top = rank < TOP_K
    kept = jnp.where(top, biased, 0.0)
    grp_score = kept
    for delta in range(-(EXP_PER_GROUP - 1), EXP_PER_GROUP):
        if delta == 0:
            continue
        in_grp = (pos + delta >= 0) & (pos + delta < EXP_PER_GROUP)
        grp_score = grp_score + jnp.where(in_grp, _lane_shift(kept, delta), 0.0)
    grank = jnp.zeros(scores.shape, jnp.int32)
    for dg in range(-(N_GROUPS - 1), N_GROUPS):
        if dg == 0:
            continue
        other = _lane_shift(grp_score, dg * EXP_PER_GROUP)
        ahead = (other > grp_score) | ((other == grp_score) & (dg < 0))
        in_rng = (grp + dg >= 0) & (grp + dg < N_GROUPS)
        grank = grank + (ahead & in_rng).astype(jnp.int32)
    sel = top & (grank == 0) & (lane < N_EXPERTS)
    w = jnp.where(sel, scores, 0.0)
    return w / jnp.sum(w, axis=-1, keepdims=True), sel


def _router_kernel(x_ref, mod_ref, g2_ref, wr_ref, br_ref, p_ref, idx_ref, cnt_ref,
                   before_scr, run_scr):
    tm, d = x_ref.shape
    half = d // 2
    i = pl.program_id(0)

    @pl.when(i == 0)
    def _init():
        row = lax.broadcasted_iota(jnp.int32, (tm, tm), 0)
        col = lax.broadcasted_iota(jnp.int32, (tm, tm), 1)
        before_scr[...] = (row < col).astype(BF16)
        run_scr[...] = jnp.zeros_like(run_scr)

    x = x_ref[...]
    modv = mod_ref[0, 0]
    sh2, sc2 = modv[3:4], modv[4:5]
    ms = jnp.mean(x * x, axis=-1, keepdims=True)
    h = (x * lax.rsqrt(ms + EPS)) * g2_ref[...] * (1.0 + sc2) + sh2
    h_hi = h.astype(BF16)
    h_hi32 = h_hi.astype(F32)
    h_lo = (h - h_hi32).astype(BF16)
    wr = wr_ref[...]
    w_hi = wr.astype(BF16)
    w_lo = (wr - w_hi.astype(F32)).astype(BF16)
    logits = (jnp.dot(h_hi, w_hi, preferred_element_type=F32)
              + jnp.dot(h_lo, w_hi, preferred_element_type=F32)
              + jnp.dot(h_hi, w_lo, preferred_element_type=F32))
    scores = _sigmoid(logits)
    comb, sel = _route(scores, scores + br_ref[...])

    bits = pltpu.bitcast(h_hi32, jnp.uint32)
    p_ref[:, 0:half] = (bits[:, 0:half] >> 16) | bits[:, half:d]
    p_ref[:, half:half + LANES] = pltpu.bitcast(comb, jnp.uint32)

    sel_t = sel.astype(F32).T
    rows = [sel_t[e:e + 1, :] for e in range(N_EXPERTS)]
    bucket = jnp.zeros((1, tm), F32)
    for g in range(N_GROUPS):
        grp_rows = rows[g * EXP_PER_GROUP:(g + 1) * EXP_PER_GROUP]
        bucket = bucket + float(g * N_PAIRS) * functools.reduce(jnp.maximum, grp_rows)
        for p, (a, b) in enumerate(PAIRS):
            if p:
                bucket = bucket + float(p) * (grp_rows[a] * grp_rows[b])
    bucket = bucket.astype(jnp.int32)
    onehot = (lax.broadcasted_iota(jnp.int32, (BUCKET_ROWS, tm), 0) == bucket).astype(F32)
    before = jnp.dot(onehot.astype(BF16), before_scr[...], preferred_element_type=F32)
    run = run_scr[...]
    rank = jnp.sum(onehot * (before + run[:, 0:1]), axis=0, keepdims=True)
    run = run + jnp.sum(onehot, axis=1, keepdims=True)
    run_scr[...] = run
    idx_ref[0] = jnp.concatenate(
        [bucket, rank.astype(jnp.int32), jnp.zeros((SUBLANES - 2, tm), jnp.int32)], axis=0)
    cnt_ref[...] = run.astype(jnp.int32)


def _router(x2d, mod, g2, wr_pad, br_pad, *, layer, seq):
    n_tok, d = x2d.shape
    tm = ROUTE_TILE
    n_tiles = n_tok // tm
    row_w = d // 2 + LANES
    return pl.pallas_call(
        _router_kernel,
        grid=(n_tiles,),
        in_specs=[
            pl.BlockSpec((tm, d), lambda i: (i, 0)),
            pl.BlockSpec((1, 1, 6, d), lambda i: (layer, (i * tm) // seq, 0, 0)),
            pl.BlockSpec((1, d), lambda i: (0, 0)),
            pl.BlockSpec((d, LANES), lambda i: (0, 0)),
            pl.BlockSpec((1, LANES), lambda i: (0, 0)),
        ],
        out_specs=[
            pl.BlockSpec((tm, row_w), lambda i: (i, 0)),
            pl.BlockSpec((1, SUBLANES, tm), lambda i: (i, 0, 0)),
            pl.BlockSpec((BUCKET_ROWS, LANES), lambda i: (0, 0)),
        ],
        out_shape=[
            jax.ShapeDtypeStruct((n_tok, row_w), jnp.uint32),
            jax.ShapeDtypeStruct((n_tiles, SUBLANES, tm), jnp.int32),
            jax.ShapeDtypeStruct((BUCKET_ROWS, LANES), jnp.int32),
        ],
        scratch_shapes=[
            pltpu.VMEM((tm, tm), BF16),
            pltpu.VMEM((BUCKET_ROWS, LANES), F32),
        ],
        compiler_params=pltpu.CompilerParams(
            dimension_semantics=("arbitrary",), vmem_limit_bytes=VMEM_LIMIT_BYTES),
        name=f"moe_router{layer}",
    )(x2d, mod, g2, wr_pad, br_pad)


def _dispatch_kernel(dest_ref, p_ref, init_ref, hs_ref, sem):
    del init_ref
    td = p_ref.shape[0]

    def issue(i, carry):
        for prio in range(DMA_PRIORITIES):
            r = i * DMA_PRIORITIES + prio
            pltpu.make_async_copy(p_ref.at[pl.ds(r, 1), :],
                                  hs_ref.at[pl.ds(dest_ref[0, 0, r], 1), :],
                                  sem).start(priority=prio)
        return carry

    lax.fori_loop(0, td // DMA_PRIORITIES, issue, 0, unroll=4)

    def drain(r, carry):
        pltpu.make_async_copy(p_ref.at[pl.ds(0, 1), :], hs_ref.at[pl.ds(0, 1), :], sem).wait()
        return carry

    lax.fori_loop(0, td, drain, 0, unroll=8)


def _dispatch(p, dest2d, n_sorted, *, layer):
    n_tok, row_w = p.shape
    td = dest2d.shape[2]
    init = jnp.zeros((n_sorted, row_w), jnp.uint32)
    return pl.pallas_call(
        _dispatch_kernel,
        grid=(n_tok // td,),
        in_specs=[
            pl.BlockSpec((1, 1, td), lambda i: (i, 0, 0), memory_space=pltpu.SMEM),
            pl.BlockSpec((td, row_w), lambda i: (i, 0)),
            pl.BlockSpec(memory_space=pl.ANY),
        ],
        out_specs=pl.BlockSpec(memory_space=pl.ANY),
        out_shape=jax.ShapeDtypeStruct((n_sorted, row_w), jnp.uint32),
        scratch_shapes=[pltpu.SemaphoreType.DMA(())],
        input_output_aliases={2: 0},
        compiler_params=pltpu.CompilerParams(
            dimension_semantics=("arbitrary",), vmem_limit_bytes=VMEM_LIMIT_BYTES),
        name=f"moe_dispatch{layer}",
    )(dest2d, p, init)


def _expert_kernel(e1_ref, e2_ref, valid_ref, hs_ref, wg1_ref, wu1_ref, wd1_ref,
                   wg2_ref, wu2_ref, wd2_ref, y_ref):
    j = pl.program_id(0)
    half = hs_ref.shape[1] - LANES

    @pl.when(valid_ref[j] != 0)
    def _compute():
        packed = hs_ref[:, 0:half]
        lo = pltpu.bitcast(packed << 16, F32)
        hi = pltpu.bitcast(packed & jnp.uint32(0xFFFF0000), F32)
        h = jnp.concatenate([lo, hi], axis=1).astype(BF16)
        comb = pltpu.bitcast(hs_ref[:, half:half + LANES], F32)
        lane = lax.broadcasted_iota(jnp.int32, comb.shape, 1)

        def expert(e, wg_ref, wu_ref, wd_ref):
            ce = jnp.sum(jnp.where(lane == e, comb, 0.0), axis=-1, keepdims=True)
            hg = jnp.dot(h, wg_ref[0], preferred_element_type=F32)
            hu = jnp.dot(h, wu_ref[0], preferred_element_type=F32)
            return ce * _bdot(_silu(hg) * hu, wd_ref[0])

        y_ref[...] = (expert(e1_ref[j], wg1_ref, wu1_ref, wd1_ref)
                      + expert(e2_ref[j], wg2_ref, wu2_ref, wd2_ref))

    @pl.when(valid_ref[j] == 0)
    def _empty():
        y_ref[...] = jnp.zeros_like(y_ref)


def _experts(hs, e1, e2, valid, w_gate, w_up, w_down, *, layer):
    n_sorted, row_w = hs.shape
    _, d, d_exp = w_gate.shape
    ts = SORT_TILE
    first = lambda j, e1, e2, valid: (e1[j], 0, 0)
    second = lambda j, e1, e2, valid: (e2[j], 0, 0)
    grid_spec = pltpu.PrefetchScalarGridSpec(
        num_scalar_prefetch=3,
        grid=(n_sorted // ts,),
        in_specs=[
            pl.BlockSpec((ts, row_w), lambda j, e1, e2, valid: (j, 0)),
            pl.BlockSpec((1, d, d_exp), first),
            pl.BlockSpec((1, d, d_exp), first),
            pl.BlockSpec((1, d_exp, d), first),
            pl.BlockSpec((1, d, d_exp), second),
            pl.BlockSpec((1, d, d_exp), second),
            pl.BlockSpec((1, d_exp, d), second),
        ],
        out_specs=pl.BlockSpec((ts, d), lambda j, e1, e2, valid: (j, 0)),
    )
    return pl.pallas_call(
        _expert_kernel,
        grid_spec=grid_spec,
        out_shape=jax.ShapeDtypeStruct((n_sorted, d), F32),
        compiler_params=pltpu.CompilerParams(
            dimension_semantics=("arbitrary",), vmem_limit_bytes=VMEM_LIMIT_BYTES),
        name=f"moe_experts{layer}",
    )(e1, e2, valid, hs, w_gate, w_up, w_down, w_gate, w_up, w_down)


def _combine_kernel(dest_ref, x_ref, mod_ref, gf_ref, ys_ref, o_ref, y_scr, sem, *, final_norm):
    tc = x_ref.shape[0]

    def issue(i, carry):
        for prio in range(DMA_PRIORITIES):
            r = i * DMA_PRIORITIES + prio
            pltpu.make_async_copy(ys_ref.at[pl.ds(dest_ref[0, 0, r], 1), :],
                                  y_scr.at[pl.ds(r, 1), :], sem).start(priority=prio)
        return carry

    lax.fori_loop(0, tc // DMA_PRIORITIES, issue, 0, unroll=4)

    def drain(r, carry):
        pltpu.make_async_copy(ys_ref.at[pl.ds(0, 1), :], y_scr.at[pl.ds(0, 1), :], sem).wait()
        return carry

    lax.fori_loop(0, tc, drain, 0, unroll=8)

    gt2 = mod_ref[0, 0][5:6]
    out = x_ref[...] + gt2 * y_scr[...]
    if final_norm:
        ms = jnp.mean(out * out, axis=-1, keepdims=True)
        out = out * lax.rsqrt(ms + EPS) * gf_ref[...]
    o_ref[...] = out


def _combine(x2d, mod, g_final, ys, dest2d, *, layer, seq, final_norm):
    n_tok, d = x2d.shape
    tc = dest2d.shape[2]
    kern = functools.partial(_combine_kernel, final_norm=final_norm)
    return pl.pallas_call(
        kern,
        grid=(n_tok // tc,),
        in_specs=[
            pl.BlockSpec((1, 1, tc), lambda i: (i, 0, 0), memory_space=pltpu.SMEM),
            pl.BlockSpec((tc, d), lambda i: (i, 0)),
            pl.BlockSpec((1, 1, 6, d), lambda i: (layer, (i * tc) // seq, 0, 0)),
            pl.BlockSpec((1, d), lambda i: (0, 0)),
            pl.BlockSpec(memory_space=pl.ANY),
        ],
        out_specs=pl.BlockSpec((tc, d), lambda i: (i, 0)),
        out_shape=jax.ShapeDtypeStruct((n_tok, d), F32),
        scratch_shapes=[pltpu.VMEM((tc, d), F32), pltpu.SemaphoreType.DMA(())],
        compiler_params=pltpu.CompilerParams(
            dimension_semantics=("arbitrary",), vmem_limit_bytes=VMEM_LIMIT_BYTES),
        name=f"moe_combine{layer}",
    )(dest2d, x2d, mod, g_final, ys)


def _moe(x2d, mod, g2, wr_pad, br_pad, w_gate, w_up, w_down, g_final, *, layer, seq, final_norm):
    n_tok, _ = x2d.shape
    p, idx, cnt = _router(x2d, mod, g2, wr_pad, br_pad, layer=layer, seq=seq)

    n_buckets = N_GROUPS * N_PAIRS
    counts = cnt[:n_buckets, 0]
    padded = ((counts + SORT_TILE - 1) // SORT_TILE) * SORT_TILE
    ends = jnp.cumsum(padded)
    starts = ends - padded
    dest = starts[idx[:, 0, :].reshape(n_tok)] + idx[:, 1, :].reshape(n_tok)
    n_tiles = n_tok // SORT_TILE + n_buckets
    tile_start = jnp.arange(n_tiles, dtype=jnp.int32) * SORT_TILE
    tile_bucket = jnp.sum((ends[None, :] <= tile_start[:, None]).astype(jnp.int32), axis=1)
    valid = (tile_bucket < n_buckets).astype(jnp.int32)
    tb = jnp.minimum(tile_bucket, n_buckets - 1)
    grp, pair = tb // N_PAIRS, tb % N_PAIRS
    pair_lo = jnp.asarray([a for a, _ in PAIRS], jnp.int32)
    pair_hi = jnp.asarray([b for _, b in PAIRS], jnp.int32)
    e1 = grp * EXP_PER_GROUP + pair_lo[pair]
    e2 = grp * EXP_PER_GROUP + pair_hi[pair]

    hs = _dispatch(p, dest.reshape(n_tok // DISPATCH_TILE, 1, DISPATCH_TILE),
                   n_tiles * SORT_TILE, layer=layer)
    ys = _experts(hs, e1, e2, valid, w_gate, w_up, w_down, layer=layer)
    return _combine(x2d, mod, g_final, ys, dest.reshape(n_tok // COMBINE_TILE, 1, COMBINE_TILE),
                    layer=layer, seq=seq, final_norm=final_norm)


def _rope_tables(seq):
    half = RB_DK // 2
    inv = ROPE_BASE ** (-jnp.arange(half, dtype=F32) / half)
    ang = jnp.arange(seq, dtype=F32)[:, None] * inv[None, :]
    cos, sin = jnp.cos(ang), jnp.sin(ang)
    return jnp.concatenate([cos, cos], axis=-1), jnp.concatenate([-sin, sin], axis=-1)


def kernel(x, c, w_ada, b_ada, g_norm1, g_norm2, w_in, lb_logits, g_hgrn, g_ret, w_branch_a,
           w_branch_b, w_out, w_router, b_router, w_exp_gate, w_exp_up, w_exp_down, g_final):
    bsz, seq, d = x.shape
    depth = w_ada.shape[0]
    assert seq % MIX_TILE == 0 and MIX_TILE % CHUNK == 0
    assert seq % ROUTE_TILE == 0 and seq % COMBINE_TILE == 0
    assert (bsz * seq) % DISPATCH_TILE == 0 and (bsz * seq) % SORT_TILE == 0
    assert w_ada.shape[2] == 6 * d and w_ada.shape[2] % MOD_COLS == 0
    assert N_GROUPS * N_PAIRS <= BUCKET_ROWS and TOP_K == 2

    mod = _modulation(c, w_ada, b_ada).reshape(depth, bsz, 6, d)
    cos_t, sin_t = _rope_tables(seq)
    wr_pad = jnp.pad(w_router, ((0, 0), (0, LANES - N_EXPERTS)))
    br_pad = jnp.pad(b_router.astype(F32), (0, LANES - N_EXPERTS)).reshape(1, LANES)
    lbl = lb_logits.astype(F32)

    for l in range(depth):
        x = _mixer_layer(x, mod, g_norm1[l].reshape(1, d), w_in[l].astype(BF16), lbl,
                         g_hgrn[l].reshape(1, HA_V), g_ret[l].reshape(1, RB_V),
                         w_branch_a[l].astype(BF16), w_branch_b[l].astype(BF16),
                         w_out[l].astype(BF16), cos_t, sin_t, layer=l)
        x = _moe(x.reshape(bsz * seq, d), mod, g_norm2[l].reshape(1, d), wr_pad, br_pad,
                 w_exp_gate[l].astype(BF16), w_exp_up[l].astype(BF16),
                 w_exp_down[l].astype(BF16), g_final.reshape(1, d),
                 layer=l, seq=seq, final_norm=(l == depth - 1)).reshape(bsz, seq, d)
    return x
```

```python
import functools
import math

import jax
import jax.numpy as jnp
from jax import lax
from jax.experimental import pallas as pl
from jax.experimental.pallas import tpu as pltpu

F32 = jnp.float32
BF16 = jnp.bfloat16

EPS = 1e-6
CHUNK = 64
ROPE_BASE = 10000.0
HA_HEADS, HA_DK, HA_DV = 4, 128, 128
RB_HEADS, RB_DK, RB_DV = 4, 128, 256
HA_K, HA_V = HA_HEADS * HA_DK, HA_HEADS * HA_DV
RB_K, RB_V = RB_HEADS * RB_DK, RB_HEADS * RB_DV
N_EXPERTS, N_GROUPS, TOP_K = 16, 4, 2
EXP_PER_GROUP = N_EXPERTS // N_GROUPS

LANES = 128
SUBLANES = 8
VMEM_LIMIT_BYTES = 56 * 1024 * 1024

MIX_TILE = 256
FAST_BLOCK = 32
SAFE_PEAK = 1e37
ROUTE_TILE = 1024
DISPATCH_TILE = 2048
SORT_TILE = 512
COMBINE_TILE = 1024
MOD_COLS = 1536
DMA_PRIORITIES = 2

PAIRS = tuple((a, b) for a in range(EXP_PER_GROUP) for b in range(a + 1, EXP_PER_GROUP))
N_PAIRS = len(PAIRS)
BUCKET_ROWS = 32

_NT = (((1,), (1,)), ((), ()))
_TN = (((0,), (0,)), ((), ()))


def _sigmoid(v):
    return jax.nn.sigmoid(v)


def _silu(v):
    return v * jax.nn.sigmoid(v)


def _bdot(a, b, dims=None):
    a = a.astype(BF16)
    b = b.astype(BF16)
    if dims is None:
        return jnp.dot(a, b, preferred_element_type=F32)
    return lax.dot_general(a, b, dims, preferred_element_type=F32)


def _mod_kernel(c_ref, w_ref, b_ref, o_ref):
    o_ref[0] = _bdot(_silu(c_ref[...]), w_ref[0]) + b_ref[0]


def _modulation(c, w_ada, b_ada):
    depth, d, six_d = w_ada.shape
    bsz = c.shape[0]
    return pl.pallas_call(
        _mod_kernel,
        grid=(depth, six_d // MOD_COLS),
        in_specs=[
            pl.BlockSpec((bsz, d), lambda l, j: (0, 0)),
            pl.BlockSpec((1, d, MOD_COLS), lambda l, j: (l, 0, j)),
            pl.BlockSpec((1, 1, MOD_COLS), lambda l, j: (l, 0, j)),
        ],
        out_specs=pl.BlockSpec((1, bsz, MOD_COLS), lambda l, j: (l, 0, j)),
        out_shape=jax.ShapeDtypeStruct((depth, bsz, six_d), F32),
        compiler_params=pltpu.CompilerParams(
            dimension_semantics=("arbitrary", "arbitrary"),
            vmem_limit_bytes=VMEM_LIMIT_BYTES),
        name="adaln_modulation",
    )(c, w_ada, b_ada.reshape(depth, 1, six_d))


def _level_ref(cum_scr, cumh, cols, block):
    t = cumh.shape[0]
    half = block // 2
    if block >= SUBLANES:
        pieces = []
        for m in range(t // block):
            r = m * block + half - 1
            pieces.append(jnp.broadcast_to(cum_scr[r:r + 1, cols], (block, cumh.shape[1])))
        return pieces[0] if len(pieces) == 1 else jnp.concatenate(pieces, axis=0)
    pos = lax.broadcasted_iota(jnp.int32, cumh.shape, 0) & (block - 1)
    ref = cumh
    for off in range(-half, half):
        if off == 0:
            continue
        shifted = pltpu.roll(cumh, (-off) % t, 0)
        ref = jnp.where(pos == half - 1 - off, shifted, ref)
    return ref


def _project(x_ref, mod_ref, g1_ref, win_ref, z_scr):
    x = x_ref[0]
    modv = mod_ref[0, 0]
    sh1, sc1 = modv[0:1], modv[1:2]
    ms = jnp.mean(x * x, axis=-1, keepdims=True)
    hn = (x * lax.rsqrt(ms + EPS)) * g1_ref[...]
    hb = (hn * (1.0 + sc1) + sh1).astype(BF16)
    yield
    n_in = win_ref.shape[1]
    seg = 1024
    for s in range(n_in // seg):
        z_scr[:, s * seg:(s + 1) * seg] = jnp.dot(
            hb, win_ref[:, s * seg:(s + 1) * seg], preferred_element_type=F32)
        yield


def _mix(z_scr, x_ref, mod_ref, first, lbl_ref, ghg_ref, gret_ref, wa_ref, wb_ref, wo_ref,
         cos_ref, sin_ref, o_ref, flag_ref, sa_scr, sb_scr, cum_scr, k_scr, retw_scr, lvl_scr,
         tri_scr, oa_scr, ob_scr, *, layer, depth, base_block):
    t = x_ref.shape[1]
    n_levels = t.bit_length() - 1
    x = x_ref[0]
    gt1 = mod_ref[0, 0][2:3]

    rows = [lbl_ref[i:i + 1, :] for i in range(depth)]
    mx = functools.reduce(jnp.maximum, rows)
    ex = [jnp.exp(r - mx) for r in rows]
    tot = functools.reduce(lambda a, b: a + b, ex)
    lb = functools.reduce(lambda a, b: a + b, [e / tot for e in ex[:layer + 1]]) - ex[0] / tot

    o_q, o_f, o_i, o_og = 0, HA_K, 2 * HA_K, 2 * HA_K + HA_V
    o_qb = o_og + HA_V
    o_kb = o_qb + RB_K
    o_vb = o_kb + RB_K
    o_ogb = o_vb + RB_V
    o_ma = o_ogb + RB_V
    o_mb = o_ma + x.shape[1]

    f = lb + (1.0 - lb) * _sigmoid(z_scr[:, o_f:o_f + HA_K])
    k_scr[...] = 1.0 - f
    g = jnp.log(f)
    g_hi = g.astype(BF16)
    g_lo = (g - g_hi.astype(F32)).astype(BF16)
    tri = tri_scr[...]
    cum_scr[...] = (jnp.dot(tri, g_hi, preferred_element_type=F32)
                    + jnp.dot(tri, g_lo, preferred_element_type=F32))

    yield

    lvl = lvl_scr[...]
    base_level = base_block.bit_length() - 1
    peak = jnp.zeros((1, HA_DK), F32)
    for h in range(HA_HEADS):
        cols = slice(h * HA_DK, (h + 1) * HA_DK)
        q = _silu(z_scr[:, o_q + h * HA_DK:o_q + (h + 1) * HA_DK])
        k = k_scr[:, cols]
        v = z_scr[:, o_i + h * HA_DV:o_i + (h + 1) * HA_DV].astype(BF16)
        cumh = cum_scr[:, cols]
        scores = jnp.zeros((t, t), F32)
        for level in range(base_level, n_levels + 1):
            if level > base_level:
                ref = _level_ref(cum_scr, cumh, cols, 1 << level)
                e = jnp.exp(-jnp.abs(cumh - ref))
                qs, ks = q * e, k * e
                mask = lvl == level
            elif base_block == 1:
                qs, ks = q, k
                mask = lvl == 0
            else:
                shift = cumh - _level_ref(cum_scr, cumh, cols, base_block)
                qs, ks = q * jnp.exp(shift), k * jnp.exp(-shift)
                big = jnp.maximum(jnp.abs(qs), jnp.abs(ks))
                peak = jnp.maximum(peak, jnp.max(big, axis=0, keepdims=True))
                mask = (lvl >= 0) & (lvl <= base_level)
            scores = jnp.where(mask, _bdot(qs, ks, _NT), scores)
        st = jnp.where(first, 0.0, sa_scr[h])
        last = cum_scr[t - 1:t, cols]
        o = _bdot(scores, v) + _bdot(q * jnp.exp(cumh), st, _NT)
        sa_scr[h] = st * jnp.exp(last) + _bdot(v, k * jnp.exp(last - cumh), _TN)
        on = o * lax.rsqrt(jnp.mean(o * o, axis=-1, keepdims=True) + EPS) * ghg_ref[:, cols]
        og = z_scr[:, o_og + h * HA_DV:o_og + (h + 1) * HA_DV]
        oa_scr[:, cols] = (on * _silu(og)).astype(BF16)
        yield

    flag_ref[0] = jnp.broadcast_to(peak, (SUBLANES, HA_DK))

    cosf = cos_ref[...]
    sins = sin_ref[...]
    pos = lax.broadcasted_iota(jnp.int32, (t, RB_DK), 0).astype(F32)
    for h in range(RB_HEADS):
        log_gamma = math.log(1.0 - 2.0 ** (-5.0 - h))
        qb = z_scr[:, o_qb + h * RB_DK:o_qb + (h + 1) * RB_DK]
        kb = z_scr[:, o_kb + h * RB_DK:o_kb + (h + 1) * RB_DK]
        qr = (qb * cosf + pltpu.roll(qb, RB_DK // 2, 1) * sins) * (RB_DK ** -0.5)
        kr = kb * cosf + pltpu.roll(kb, RB_DK // 2, 1) * sins
        vcols = slice(h * RB_DV, (h + 1) * RB_DV)
        vb = z_scr[:, o_vb + h * RB_DV:o_vb + (h + 1) * RB_DV].astype(BF16)
        s = _bdot(qr, kr, _NT) * retw_scr[h]
        sb = jnp.where(first, 0.0, sb_scr[h])
        q_dec = jnp.exp(log_gamma * (pos + 1.0))
        k_dec = jnp.exp(log_gamma * (float(t - 1) - pos))
        o = _bdot(s, vb) + _bdot(qr * q_dec, sb)
        sb_scr[h] = math.exp(log_gamma * t) * sb + _bdot(kr * k_dec, vb, _TN)
        mu = jnp.mean(o, axis=-1, keepdims=True)
        oc = o - mu
        var = jnp.mean(oc * oc, axis=-1, keepdims=True)
        on = oc * lax.rsqrt(var + EPS) * gret_ref[:, vcols]
        og = z_scr[:, o_ogb + h * RB_DV:o_ogb + (h + 1) * RB_DV]
        ob_scr[:, vcols] = (on * _silu(og)).astype(BF16)
        yield

    d = x.shape[1]
    ya = jnp.dot(oa_scr[...], wa_ref[...], preferred_element_type=F32)
    yb = jnp.dot(ob_scr[...], wb_ref[...], preferred_element_type=F32)
    y = _sigmoid(z_scr[:, o_ma:o_ma + d]) * ya + _sigmoid(z_scr[:, o_mb:o_mb + d]) * yb
    o_ref[0] = x + gt1 * _bdot(y, wo_ref[...])


def _mixer_kernel(xp_ref, xm_ref, modp_ref, modm_ref, g1_ref, win_ref, lbl_ref, ghg_ref,
                  gret_ref, wa_ref, wb_ref, wo_ref, cos_ref, sin_ref, o_ref, flag_ref,
                  z0_scr, z1_scr, sa_scr, sb_scr, cum_scr, k_scr, retw_scr, lvl_scr, tri_scr,
                  oa_scr, ob_scr, *, layer, depth, tiles_per_seq, base_block):
    t = xp_ref.shape[1]
    n_levels = t.bit_length() - 1
    g = pl.program_id(0)

    @pl.when(g == 0)
    def _build_constants():
        row = lax.broadcasted_iota(jnp.int32, (t, t), 0)
        col = lax.broadcasted_iota(jnp.int32, (t, t), 1)
        diff = row ^ col
        lvl = jnp.zeros((t, t), jnp.int32)
        for bit in range(n_levels):
            lvl = lvl + (diff >= (1 << bit)).astype(jnp.int32)
        lvl_scr[...] = jnp.where(col > row, -1, lvl)
        tri_scr[...] = (col <= row).astype(BF16)
        dist = jnp.abs(row - col).astype(F32)
        visible = (col // CHUNK) <= (row // CHUNK)
        for h in range(RB_HEADS):
            log_gamma = math.log(1.0 - 2.0 ** (-5.0 - h))
            retw_scr[h] = jnp.where(visible, jnp.exp(log_gamma * dist), 0.0)
        z1_scr[...] = jnp.zeros_like(z1_scr)
        sa_scr[...] = jnp.zeros_like(sa_scr)
        sb_scr[...] = jnp.zeros_like(sb_scr)

    first = lax.rem(jnp.maximum(g - 1, 0), tiles_per_seq) == 0

    def step(z_write, z_read):
        proj = _project(xp_ref, modp_ref, g1_ref, win_ref, z_write)
        mix = _mix(z_read, xm_ref, modm_ref, first, lbl_ref, ghg_ref, gret_ref, wa_ref, wb_ref,
                   wo_ref, cos_ref, sin_ref, o_ref, flag_ref, sa_scr, sb_scr, cum_scr, k_scr,
                   retw_scr, lvl_scr, tri_scr, oa_scr, ob_scr, layer=layer, depth=depth,
                   base_block=base_block)
        live = [proj, mix]
        while live:
            for gen in list(live):
                if next(gen, StopIteration) is StopIteration:
                    live.remove(gen)

    @pl.when(lax.rem(g, 2) == 0)
    def _even():
        step(z0_scr, z1_scr)

    @pl.when(lax.rem(g, 2) == 1)
    def _odd():
        step(z1_scr, z0_scr)


def _const_spec(shape):
    return pl.BlockSpec(shape, lambda g: (0,) * len(shape), pipeline_mode=pl.Buffered(1))


def _mixer(x, mod, g1, w_in, lb_logits, g_hgrn, g_ret, w_a, w_b, w_o, cos_t, sin_t, *, layer,
           base_block):
    bsz, seq, d = x.shape
    depth = mod.shape[0]
    t = MIX_TILE
    n_in = w_in.shape[1]
    tps = seq // t
    n_tiles = bsz * tps
    kern = functools.partial(_mixer_kernel, layer=layer, depth=depth, tiles_per_seq=tps,
                             base_block=base_block)

    def proj_tile(g):
        return jnp.minimum(g, n_tiles - 1)

    def mix_tile(g):
        return jnp.maximum(g - 1, 0)

    return pl.pallas_call(
        kern,
        grid=(n_tiles + 1,),
        in_specs=[
            pl.BlockSpec((1, t, d), lambda g: (proj_tile(g) // tps, proj_tile(g) % tps, 0)),
            pl.BlockSpec((1, t, d), lambda g: (mix_tile(g) // tps, mix_tile(g) % tps, 0)),
            pl.BlockSpec((1, 1, 6, d), lambda g: (layer, proj_tile(g) // tps, 0, 0)),
            pl.BlockSpec((1, 1, 6, d), lambda g: (layer, mix_tile(g) // tps, 0, 0)),
            _const_spec((1, d)),
            _const_spec((d, n_in)),
            _const_spec((depth, HA_K)),
            _const_spec((1, HA_V)),
            _const_spec((1, RB_V)),
            _const_spec((HA_V, d)),
            _const_spec((RB_V, d)),
            _const_spec((d, d)),
            pl.BlockSpec((t, RB_DK), lambda g: (mix_tile(g) % tps, 0)),
            pl.BlockSpec((t, RB_DK), lambda g: (mix_tile(g) % tps, 0)),
        ],
        out_specs=[
            pl.BlockSpec((1, t, d), lambda g: (mix_tile(g) // tps, mix_tile(g) % tps, 0)),
            pl.BlockSpec((1, SUBLANES, HA_DK), lambda g: (g, 0, 0)),
        ],
        out_shape=[
            jax.ShapeDtypeStruct((bsz, seq, d), F32),
            jax.ShapeDtypeStruct((n_tiles + 1, SUBLANES, HA_DK), F32),
        ],
        scratch_shapes=[
            pltpu.VMEM((t, n_in), F32),
            pltpu.VMEM((t, n_in), F32),
            pltpu.VMEM((HA_HEADS, HA_DV, HA_DK), F32),
            pltpu.VMEM((RB_HEADS, RB_DK, RB_DV), F32),
            pltpu.VMEM((t, HA_K), F32),
            pltpu.VMEM((t, HA_K), F32),
            pltpu.VMEM((RB_HEADS, t, t), F32),
            pltpu.VMEM((t, t), jnp.int32),
            pltpu.VMEM((t, t), BF16),
            pltpu.VMEM((t, HA_V), BF16),
            pltpu.VMEM((t, RB_V), BF16),
        ],
        compiler_params=pltpu.CompilerParams(
            dimension_semantics=("arbitrary",), vmem_limit_bytes=VMEM_LIMIT_BYTES),
        name=f"mixer_layer{layer}_block{base_block}",
    )(x, x, mod, mod, g1, w_in, lb_logits, g_hgrn, g_ret, w_a, w_b, w_o, cos_t, sin_t)


def _mixer_layer(x, *args, layer):
    fast, peak = _mixer(x, *args, layer=layer, base_block=FAST_BLOCK)
    safe = jnp.max(peak) <= SAFE_PEAK
    return lax.cond(safe, lambda: fast,
                    lambda: _mixer(x, *args, layer=layer, base_block=1)[0])


def _expert_shift(a, delta):
    return pltpu.roll(a, (-delta) % a.shape[0], 0)


def _route(scores, biased):
    idx = lax.broadcasted_iota(jnp.int32, scores.shape, 0)
    pos = idx & (EXP_PER_GROUP - 1)
    grp = idx // EXP_PER_GROUP
    rank = jnp.zeros(scores.shape, jnp.int32)
    for delta in range(-(EXP_PER_GROUP - 1), EXP_PER_GROUP):
        if delta == 0:
            continue
        other = _expert_shift(biased, delta)
        ahead = (other > biased) | ((other == biased) & (delta < 0))
        in_grp = (pos + delta >= 0) & (pos + delta < EXP_PER_GROUP)
        rank = rank + (ahead & in_grp).astype(jnp.int32)
    top = rank < TOP_K
    kept = jnp.where(top, biased, 0.0)
    grp_score = kept
    for delta in range(-(EXP_PER_GROUP - 1), EXP_PER_GROUP):
        if delta == 0:
            continue
        in_grp = (pos + delta >= 0) & (pos + delta < EXP_PER_GROUP)
        grp_score = grp_score + jnp.where(in_grp, _expert_shift(kept, delta), 0.0)
    grank = jnp.zeros(scores.shape, jnp.int32)
    for dg in range(-(N_GROUPS - 1), N_GROUPS):
        if dg == 0:
            continue
        other = _expert_shift(grp_score, dg * EXP_PER_GROUP)
        ahead = (other > grp_score) | ((other == grp_score) & (dg < 0))
        in_rng = (grp + dg >= 0) & (grp + dg < N_GROUPS)
        grank = grank + (ahead & in_rng).astype(jnp.int32)
    sel = top & (grank == 0)
    w = jnp.where(sel, scores, 0.0)
    return w / jnp.sum(w, axis=0, keepdims=True), sel


def _router_kernel(x_ref, mod_ref, g2_ref, wr_ref, bcol_ref, p_ref, idx_ref, cnt_ref,
                   before_scr, run_scr):
    tm, d = x_ref.shape
    half = d // 2
    i = pl.program_id(0)

    @pl.when(i == 0)
    def _init():
        row = lax.broadcasted_iota(jnp.int32, (LANES, LANES), 0)
        col = lax.broadcasted_iota(jnp.int32, (LANES, LANES), 1)
        before_scr[...] = (row < col).astype(BF16)
        run_scr[...] = jnp.zeros_like(run_scr)

    x = x_ref[...]
    modv = mod_ref[0, 0]
    sh2, sc2 = modv[3:4], modv[4:5]
    ms = jnp.mean(x * x, axis=-1, keepdims=True)
    h = (x * lax.rsqrt(ms + EPS)) * g2_ref[...] * (1.0 + sc2) + sh2
    h_hi = h.astype(BF16)
    h_hi32 = h_hi.astype(F32)
    h_lo = (h - h_hi32).astype(BF16)
    wr = wr_ref[...]
    w_hi = wr.astype(BF16)
    w_lo = (wr - w_hi.astype(F32)).astype(BF16)
    logits = (jnp.dot(h_hi, w_hi, preferred_element_type=F32)
              + jnp.dot(h_lo, w_hi, preferred_element_type=F32)
              + jnp.dot(h_hi, w_lo, preferred_element_type=F32))
    scores = _sigmoid(logits.T[0:N_EXPERTS, :])
    comb, sel = _route(scores, scores + bcol_ref[...])

    bits = pltpu.bitcast(h_hi32, jnp.uint32)
    p_ref[:, 0:half] = (bits[:, 0:half] >> 16) | bits[:, half:d]
    comb_rows = jnp.concatenate([comb, jnp.zeros((LANES - N_EXPERTS, tm), F32)], axis=0).T
    p_ref[:, half:half + LANES] = pltpu.bitcast(comb_rows, jnp.uint32)

    sel_f = sel.astype(F32)
    rows = [sel_f[e:e + 1, :] for e in range(N_EXPERTS)]
    bucket = jnp.zeros((1, tm), F32)
    for g in range(N_GROUPS):
        grp_rows = rows[g * EXP_PER_GROUP:(g + 1) * EXP_PER_GROUP]
        bucket = bucket + float(g * N_PAIRS) * functools.reduce(jnp.maximum, grp_rows)
        for p, (a, b) in enumerate(PAIRS):
            if p:
                bucket = bucket + float(p) * (grp_rows[a] * grp_rows[b])
    bucket = bucket.astype(jnp.int32)
    onehot = (lax.broadcasted_iota(jnp.int32, (BUCKET_ROWS, tm), 0) == bucket).astype(F32)
    chunks = [onehot[:, c * LANES:(c + 1) * LANES] for c in range(tm // LANES)]
    inside = jnp.dot(jnp.concatenate(chunks, axis=0).astype(BF16), before_scr[...],
                     preferred_element_type=F32)
    run = run_scr[...]
    seen = run[:, 0:1]
    ranks = []
    for c, chunk in enumerate(chunks):
        earlier = inside[c * BUCKET_ROWS:(c + 1) * BUCKET_ROWS, :] + seen
        ranks.append(jnp.sum(chunk * earlier, axis=0, keepdims=True))
        seen = seen + jnp.sum(chunk, axis=1, keepdims=True)
    rank = jnp.concatenate(ranks, axis=1)
    run = jnp.broadcast_to(seen, run.shape)
    run_scr[...] = run
    idx_ref[0] = jnp.concatenate(
        [bucket, rank.astype(jnp.int32), jnp.zeros((SUBLANES - 2, tm), jnp.int32)], axis=0)
    cnt_ref[...] = run.astype(jnp.int32)


def _router(x2d, mod, g2, wr_pad, b_col, *, layer, seq):
    n_tok, d = x2d.shape
    tm = ROUTE_TILE
    n_tiles = n_tok // tm
    row_w = d // 2 + LANES
    return pl.pallas_call(
        _router_kernel,
        grid=(n_tiles,),
        in_specs=[
            pl.BlockSpec((tm, d), lambda i: (i, 0)),
            pl.BlockSpec((1, 1, 6, d), lambda i: (layer, (i * tm) // seq, 0, 0)),
            pl.BlockSpec((1, d), lambda i: (0, 0)),
            pl.BlockSpec((d, LANES), lambda i: (0, 0)),
            pl.BlockSpec((N_EXPERTS, 1), lambda i: (0, 0)),
        ],
        out_specs=[
            pl.BlockSpec((tm, row_w), lambda i: (i, 0)),
            pl.BlockSpec((1, SUBLANES, tm), lambda i: (i, 0, 0)),
            pl.BlockSpec((BUCKET_ROWS, LANES), lambda i: (0, 0)),
        ],
        out_shape=[
            jax.ShapeDtypeStruct((n_tok, row_w), jnp.uint32),
            jax.ShapeDtypeStruct((n_tiles, SUBLANES, tm), jnp.int32),
            jax.ShapeDtypeStruct((BUCKET_ROWS, LANES), jnp.int32),
        ],
        scratch_shapes=[
            pltpu.VMEM((LANES, LANES), BF16),
            pltpu.VMEM((BUCKET_ROWS, LANES), F32),
        ],
        compiler_params=pltpu.CompilerParams(
            dimension_semantics=("arbitrary",), vmem_limit_bytes=VMEM_LIMIT_BYTES),
        name=f"moe_router{layer}",
    )(x2d, mod, g2, wr_pad, b_col)


def _dispatch_kernel(dest_ref, p_ref, init_ref, hs_ref, sem):
    del init_ref
    groups = p_ref.shape[0]

    def issue(i, carry):
        for u in range(SUBLANES):
            row = dest_ref[0, 0, i * SUBLANES + u]
            pltpu.make_async_copy(p_ref.at[i, pl.ds(u, 1), :], hs_ref.at[pl.ds(row, 1), :],
                                  sem).start(priority=u % DMA_PRIORITIES)
        return carry

    lax.fori_loop(0, groups, issue, 0)

    def drain(i, carry):
        for _ in range(SUBLANES):
            pltpu.make_async_copy(p_ref.at[0, pl.ds(0, 1), :], hs_ref.at[pl.ds(0, 1), :],
                                  sem).wait()
        return carry

    lax.fori_loop(0, groups, drain, 0)


def _dispatch(p, dest2d, n_sorted, *, layer):
    n_tok, row_w = p.shape
    td = dest2d.shape[2]
    init = jnp.zeros((n_sorted, row_w), jnp.uint32)
    p = p.reshape(n_tok // SUBLANES, SUBLANES, row_w)
    return pl.pallas_call(
        _dispatch_kernel,
        grid=(n_tok // td,),
        in_specs=[
            pl.BlockSpec((1, 1, td), lambda i: (i, 0, 0), memory_space=pltpu.SMEM),
            pl.BlockSpec((td // SUBLANES, SUBLANES, row_w), lambda i: (i, 0, 0)),
            pl.BlockSpec(memory_space=pl.ANY),
        ],
        out_specs=pl.BlockSpec(memory_space=pl.ANY),
        out_shape=jax.ShapeDtypeStruct((n_sorted, row_w), jnp.uint32),
        scratch_shapes=[pltpu.SemaphoreType.DMA(())],
        input_output_aliases={2: 0},
        compiler_params=pltpu.CompilerParams(
            dimension_semantics=("arbitrary",), vmem_limit_bytes=VMEM_LIMIT_BYTES),
        name=f"moe_dispatch{layer}",
    )(dest2d, p, init)


def _expert_kernel(e1_ref, e2_ref, valid_ref, hs_ref, wg1_ref, wu1_ref, wd1_ref,
                   wg2_ref, wu2_ref, wd2_ref, y_ref):
    j = pl.program_id(0)
    half = hs_ref.shape[1] - LANES

    @pl.when(valid_ref[j] != 0)
    def _compute():
        packed = hs_ref[:, 0:half]
        lo = pltpu.bitcast(packed << 16, F32)
        hi = pltpu.bitcast(packed & jnp.uint32(0xFFFF0000), F32)
        h = jnp.concatenate([lo, hi], axis=1).astype(BF16)
        comb = pltpu.bitcast(hs_ref[:, half:half + LANES], F32)
        lane = lax.broadcasted_iota(jnp.int32, comb.shape, 1)

        def expert(e, wg_ref, wu_ref, wd_ref):
            ce = jnp.sum(jnp.where(lane == e, comb, 0.0), axis=-1, keepdims=True)
            hg = jnp.dot(h, wg_ref[0], preferred_element_type=F32)
            hu = jnp.dot(h, wu_ref[0], preferred_element_type=F32)
            return ce * _bdot(_silu(hg) * hu, wd_ref[0])

        y_ref[...] = (expert(e1_ref[j], wg1_ref, wu1_ref, wd1_ref)
                      + expert(e2_ref[j], wg2_ref, wu2_ref, wd2_ref))

    @pl.when(valid_ref[j] == 0)
    def _empty():
        y_ref[...] = jnp.zeros_like(y_ref)


def _experts(hs, e1, e2, valid, w_gate, w_up, w_down, *, layer):
    n_sorted, row_w = hs.shape
    _, d, d_exp = w_gate.shape
    ts = SORT_TILE
    first = lambda j, e1, e2, valid: (e1[j], 0, 0)
    second = lambda j, e1, e2, valid: (e2[j], 0, 0)
    grid_spec = pltpu.PrefetchScalarGridSpec(
        num_scalar_prefetch=3,
        grid=(n_sorted // ts,),
        in_specs=[
            pl.BlockSpec((ts, row_w), lambda j, e1, e2, valid: (j, 0)),
            pl.BlockSpec((1, d, d_exp), first),
            pl.BlockSpec((1, d, d_exp), first),
            pl.BlockSpec((1, d_exp, d), first),
            pl.BlockSpec((1, d, d_exp), second),
            pl.BlockSpec((1, d, d_exp), second),
            pl.BlockSpec((1, d_exp, d), second),
        ],
        out_specs=pl.BlockSpec((ts, d), lambda j, e1, e2, valid: (j, 0)),
    )
    return pl.pallas_call(
        _expert_kernel,
        grid_spec=grid_spec,
        out_shape=jax.ShapeDtypeStruct((n_sorted, d), F32),
        compiler_params=pltpu.CompilerParams(
            dimension_semantics=("arbitrary",), vmem_limit_bytes=VMEM_LIMIT_BYTES),
        name=f"moe_experts{layer}",
    )(e1, e2, valid, hs, w_gate, w_up, w_down, w_gate, w_up, w_down)


def _combine_kernel(dest_ref, x_ref, mod_ref, gf_ref, ys_ref, o_ref, y_scr, sem, *, final_norm):
    tc, d = x_ref.shape
    groups = y_scr.shape[0]

    def issue(i, carry):
        for u in range(SUBLANES):
            row = dest_ref[0, 0, i * SUBLANES + u]
            pltpu.make_async_copy(ys_ref.at[pl.ds(row, 1), :], y_scr.at[i, pl.ds(u, 1), :],
                                  sem).start(priority=u % DMA_PRIORITIES)
        return carry

    lax.fori_loop(0, groups, issue, 0)

    def drain(i, carry):
        for _ in range(SUBLANES):
            pltpu.make_async_copy(ys_ref.at[pl.ds(0, 1), :], y_scr.at[0, pl.ds(0, 1), :],
                                  sem).wait()
        return carry

    lax.fori_loop(0, groups, drain, 0)

    gt2 = mod_ref[0, 0][5:6]
    out = x_ref[...] + gt2 * y_scr[...].reshape(tc, d)
    if final_norm:
        ms = jnp.mean(out * out, axis=-1, keepdims=True)
        out = out * lax.rsqrt(ms + EPS) * gf_ref[...]
    o_ref[...] = out


def _combine(x2d, mod, g_final, ys, dest2d, *, layer, seq, final_norm):
    n_tok, d = x2d.shape
    tc = dest2d.shape[2]
    kern = functools.partial(_combine_kernel, final_norm=final_norm)
    return pl.pallas_call(
        kern,
        grid=(n_tok // tc,),
        in_specs=[
            pl.BlockSpec((1, 1, tc), lambda i: (i, 0, 0), memory_space=pltpu.SMEM),
            pl.BlockSpec((tc, d), lambda i: (i, 0)),
            pl.BlockSpec((1, 1, 6, d), lambda i: (layer, (i * tc) // seq, 0, 0)),
            pl.BlockSpec((1, d), lambda i: (0, 0)),
            pl.BlockSpec(memory_space=pl.ANY),
        ],
        out_specs=pl.BlockSpec((tc, d), lambda i: (i, 0)),
        out_shape=jax.ShapeDtypeStruct((n_tok, d), F32),
        scratch_shapes=[pltpu.VMEM((tc // SUBLANES, SUBLANES, d), F32),
                        pltpu.SemaphoreType.DMA(())],
        compiler_params=pltpu.CompilerParams(
            dimension_semantics=("arbitrary",), vmem_limit_bytes=VMEM_LIMIT_BYTES),
        name=f"moe_combine{layer}",
    )(dest2d, x2d, mod, g_final, ys)


def _moe(x2d, mod, g2, wr_pad, b_col, w_gate, w_up, w_down, g_final, *, layer, seq, final_norm):
    n_tok, _ = x2d.shape
    p, idx, cnt = _router(x2d, mod, g2, wr_pad, b_col, layer=layer, seq=seq)

    n_buckets = N_GROUPS * N_PAIRS
    counts = cnt[:n_buckets, 0]
    padded = ((counts + SORT_TILE - 1) // SORT_TILE) * SORT_TILE
    ends = jnp.cumsum(padded)
    starts = ends - padded
    dest = starts[idx[:, 0, :].reshape(n_tok)] + idx[:, 1, :].reshape(n_tok)
    n_tiles = n_tok // SORT_TILE + n_buckets
    tile_start = jnp.arange(n_tiles, dtype=jnp.int32) * SORT_TILE
    tile_bucket = jnp.sum((ends[None, :] <= tile_start[:, None]).astype(jnp.int32), axis=1)
    valid = (tile_bucket < n_buckets).astype(jnp.int32)
    tb = jnp.minimum(tile_bucket, n_buckets - 1)
    grp, pair = tb // N_PAIRS, tb % N_PAIRS
    pair_lo = jnp.asarray([a for a, _ in PAIRS], jnp.int32)
    pair_hi = jnp.asarray([b for _, b in PAIRS], jnp.int32)
    e1 = grp * EXP_PER_GROUP + pair_lo[pair]
    e2 = grp * EXP_PER_GROUP + pair_hi[pair]

    hs = _dispatch(p, dest.reshape(n_tok // DISPATCH_TILE, 1, DISPATCH_TILE),
                   n_tiles * SORT_TILE, layer=layer)
    ys = _experts(hs, e1, e2, valid, w_gate, w_up, w_down, layer=layer)
    return _combine(x2d, mod, g_final, ys, dest.reshape(n_tok // COMBINE_TILE, 1, COMBINE_TILE),
                    layer=layer, seq=seq, final_norm=final_norm)


def _rope_tables(seq):
    half = RB_DK // 2
    inv = ROPE_BASE ** (-jnp.arange(half, dtype=F32) / half)
    ang = jnp.arange(seq, dtype=F32)[:, None] * inv[None, :]
    cos, sin = jnp.cos(ang), jnp.sin(ang)
    return jnp.concatenate([cos, cos], axis=-1), jnp.concatenate([-sin, sin], axis=-1)


def kernel(x, c, w_ada, b_ada, g_norm1, g_norm2, w_in, lb_logits, g_hgrn, g_ret, w_branch_a,
           w_branch_b, w_out, w_router, b_router, w_exp_gate, w_exp_up, w_exp_down, g_final):
    bsz, seq, d = x.shape
    depth = w_ada.shape[0]
    assert seq % MIX_TILE == 0 and MIX_TILE % CHUNK == 0
    assert seq % ROUTE_TILE == 0 and seq % COMBINE_TILE == 0
    assert (bsz * seq) % DISPATCH_TILE == 0 and (bsz * seq) % SORT_TILE == 0
    assert w_ada.shape[2] == 6 * d and w_ada.shape[2] % MOD_COLS == 0
    assert N_GROUPS * N_PAIRS <= BUCKET_ROWS and TOP_K == 2

    mod = _modulation(c, w_ada, b_ada).reshape(depth, bsz, 6, d)
    cos_t, sin_t = _rope_tables(seq)
    wr_pad = jnp.pad(w_router, ((0, 0), (0, LANES - N_EXPERTS)))
    b_col = b_router.astype(F32).reshape(N_EXPERTS, 1)
    lbl = lb_logits.astype(F32)

    for l in range(depth):
        x = _mixer_layer(x, mod, g_norm1[l].reshape(1, d), w_in[l].astype(BF16), lbl,
                         g_hgrn[l].reshape(1, HA_V), g_ret[l].reshape(1, RB_V),
                         w_branch_a[l].astype(BF16), w_branch_b[l].astype(BF16),
                         w_out[l].astype(BF16), cos_t, sin_t, layer=l)
        x = _moe(x.reshape(bsz * seq, d), mod, g_norm2[l].reshape(1, d), wr_pad, b_col,
                 w_exp_gate[l].astype(BF16), w_exp_up[l].astype(BF16),
                 w_exp_down[l].astype(BF16), g_final.reshape(1, d),
                 layer=l, seq=seq, final_norm=(l == depth - 1)).reshape(bsz, seq, d)
    return x
```

```python
import functools
import math

import jax
import jax.numpy as jnp
from jax import lax
from jax.experimental import pallas as pl
from jax.experimental.pallas import tpu as pltpu

F32 = jnp.float32
BF16 = jnp.bfloat16

EPS = 1e-6
CHUNK = 64
ROPE_BASE = 10000.0
HA_HEADS, HA_DK, HA_DV = 4, 128, 128
RB_HEADS, RB_DK, RB_DV = 4, 128, 256
HA_K, HA_V = HA_HEADS * HA_DK, HA_HEADS * HA_DV
RB_K, RB_V = RB_HEADS * RB_DK, RB_HEADS * RB_DV
N_EXPERTS, N_GROUPS, TOP_K = 16, 4, 2
EXP_PER_GROUP = N_EXPERTS // N_GROUPS

LANES = 128
SUBLANES = 8
VMEM_LIMIT_BYTES = 56 * 1024 * 1024

MIX_TILE = 256
FAST_BLOCK = 32
SAFE_PEAK = 1e37
ROUTE_TILE = 1024
DISPATCH_TILE = 2048
SORT_TILE = 512
COMBINE_TILE = 1024
MOD_COLS = 1536
DMA_PRIORITIES = 2

PAIRS = tuple((a, b) for a in range(EXP_PER_GROUP) for b in range(a + 1, EXP_PER_GROUP))
N_PAIRS = len(PAIRS)
BUCKET_ROWS = 32

_NT = (((1,), (1,)), ((), ()))
_TN = (((0,), (0,)), ((), ()))


def _sigmoid(v):
    return jax.nn.sigmoid(v)


def _silu(v):
    return v * jax.nn.sigmoid(v)


def _bdot(a, b, dims=None):
    a = a.astype(BF16)
    b = b.astype(BF16)
    if dims is None:
        return jnp.dot(a, b, preferred_element_type=F32)
    return lax.dot_general(a, b, dims, preferred_element_type=F32)


def _mod_kernel(c_ref, w_ref, b_ref, o_ref):
    o_ref[0] = _bdot(_silu(c_ref[...]), w_ref[0]) + b_ref[0]


def _modulation(c, w_ada, b_ada):
    depth, d, six_d = w_ada.shape
    bsz = c.shape[0]
    return pl.pallas_call(
        _mod_kernel,
        grid=(depth, six_d // MOD_COLS),
        in_specs=[
            pl.BlockSpec((bsz, d), lambda l, j: (0, 0)),
            pl.BlockSpec((1, d, MOD_COLS), lambda l, j: (l, 0, j)),
            pl.BlockSpec((1, 1, MOD_COLS), lambda l, j: (l, 0, j)),
        ],
        out_specs=pl.BlockSpec((1, bsz, MOD_COLS), lambda l, j: (l, 0, j)),
        out_shape=jax.ShapeDtypeStruct((depth, bsz, six_d), F32),
        compiler_params=pltpu.CompilerParams(
            dimension_semantics=("arbitrary", "arbitrary"),
            vmem_limit_bytes=VMEM_LIMIT_BYTES),
        name="adaln_modulation",
    )(c, w_ada, b_ada.reshape(depth, 1, six_d))


def _level_ref(cum_scr, cumh, cols, block):
    t = cumh.shape[0]
    half = block // 2
    if block >= SUBLANES:
        pieces = []
        for m in range(t // block):
            r = m * block + half - 1
            pieces.append(jnp.broadcast_to(cum_scr[r:r + 1, cols], (block, cumh.shape[1])))
        return pieces[0] if len(pieces) == 1 else jnp.concatenate(pieces, axis=0)
    pos = lax.broadcasted_iota(jnp.int32, cumh.shape, 0) & (block - 1)
    ref = cumh
    for off in range(-half, half):
        if off == 0:
            continue
        shifted = pltpu.roll(cumh, (-off) % t, 0)
        ref = jnp.where(pos == half - 1 - off, shifted, ref)
    return ref


def _project(x_ref, mod_ref, g1_ref, win_ref, z_scr):
    x = x_ref[0]
    modv = mod_ref[0, 0]
    sh1, sc1 = modv[0:1], modv[1:2]
    ms = jnp.mean(x * x, axis=-1, keepdims=True)
    hn = (x * lax.rsqrt(ms + EPS)) * g1_ref[...]
    hb = (hn * (1.0 + sc1) + sh1).astype(BF16)
    yield
    n_in = win_ref.shape[1]
    seg = 1024
    for s in range(n_in // seg):
        z_scr[:, s * seg:(s + 1) * seg] = jnp.dot(
            hb, win_ref[:, s * seg:(s + 1) * seg], preferred_element_type=F32)
        yield


def _mix(z_scr, x_ref, mod_ref, first, lbl_ref, ghg_ref, gret_ref, wa_ref, wb_ref, wo_ref,
         cos_ref, sin_ref, o_ref, flag_ref, sa_scr, sb_scr, cum_scr, k_scr, retw_scr, lvl_scr,
         tri_scr, oa_scr, ob_scr, *, layer, depth, base_block):
    t = x_ref.shape[1]
    n_levels = t.bit_length() - 1
    x = x_ref[0]
    gt1 = mod_ref[0, 0][2:3]

    rows = [lbl_ref[i:i + 1, :] for i in range(depth)]
    mx = functools.reduce(jnp.maximum, rows)
    ex = [jnp.exp(r - mx) for r in rows]
    tot = functools.reduce(lambda a, b: a + b, ex)
    lb = functools.reduce(lambda a, b: a + b, [e / tot for e in ex[:layer + 1]]) - ex[0] / tot

    o_q, o_f, o_i, o_og = 0, HA_K, 2 * HA_K, 2 * HA_K + HA_V
    o_qb = o_og + HA_V
    o_kb = o_qb + RB_K
    o_vb = o_kb + RB_K
    o_ogb = o_vb + RB_V
    o_ma = o_ogb + RB_V
    o_mb = o_ma + x.shape[1]

    f = lb + (1.0 - lb) * _sigmoid(z_scr[:, o_f:o_f + HA_K])
    k_scr[...] = 1.0 - f
    g = jnp.log(f)
    g_hi = g.astype(BF16)
    g_lo = (g - g_hi.astype(F32)).astype(BF16)
    tri = tri_scr[...]
    cum_scr[...] = (jnp.dot(tri, g_hi, preferred_element_type=F32)
                    + jnp.dot(tri, g_lo, preferred_element_type=F32))

    yield

    lvl = lvl_scr[...]
    base_level = base_block.bit_length() - 1
    peak = jnp.zeros((1, HA_DK), F32)
    for h in range(HA_HEADS):
        cols = slice(h * HA_DK, (h + 1) * HA_DK)
        q = _silu(z_scr[:, o_q + h * HA_DK:o_q + (h + 1) * HA_DK])
        k = k_scr[:, cols]
        v = z_scr[:, o_i + h * HA_DV:o_i + (h + 1) * HA_DV].astype(BF16)
        cumh = cum_scr[:, cols]
        scores = jnp.zeros((t, t), F32)
        for level in range(base_level, n_levels + 1):
            if level > base_level:
                ref = _level_ref(cum_scr, cumh, cols, 1 << level)
                e = jnp.exp(-jnp.abs(cumh - ref))
                qs, ks = q * e, k * e
                mask = lvl == level
            elif base_block == 1:
                qs, ks = q, k
                mask = lvl == 0
            else:
                shift = cumh - _level_ref(cum_scr, cumh, cols, base_block)
                qs, ks = q * jnp.exp(shift), k * jnp.exp(-shift)
                big = jnp.maximum(jnp.abs(qs), jnp.abs(ks))
                peak = jnp.maximum(peak, jnp.max(big, axis=0, keepdims=True))
                mask = (lvl >= 0) & (lvl <= base_level)
            scores = jnp.where(mask, _bdot(qs, ks, _NT), scores)
        st = jnp.where(first, 0.0, sa_scr[h])
        last = cum_scr[t - 1:t, cols]
        o = _bdot(scores, v) + _bdot(q * jnp.exp(cumh), st, _NT)
        sa_scr[h] = st * jnp.exp(last) + _bdot(v, k * jnp.exp(last - cumh), _TN)
        on = o * lax.rsqrt(jnp.mean(o * o, axis=-1, keepdims=True) + EPS) * ghg_ref[:, cols]
        og = z_scr[:, o_og + h * HA_DV:o_og + (h + 1) * HA_DV]
        oa_scr[:, cols] = (on * _silu(og)).astype(BF16)
        yield

    flag_ref[0] = jnp.broadcast_to(peak, (SUBLANES, HA_DK))

    cosf = cos_ref[...]
    sins = sin_ref[...]
    pos = lax.broadcasted_iota(jnp.int32, (t, RB_DK), 0).astype(F32)
    for h in range(RB_HEADS):
        log_gamma = math.log(1.0 - 2.0 ** (-5.0 - h))
        qb = z_scr[:, o_qb + h * RB_DK:o_qb + (h + 1) * RB_DK]
        kb = z_scr[:, o_kb + h * RB_DK:o_kb + (h + 1) * RB_DK]
        qr = (qb * cosf + pltpu.roll(qb, RB_DK // 2, 1) * sins) * (RB_DK ** -0.5)
        kr = kb * cosf + pltpu.roll(kb, RB_DK // 2, 1) * sins
        vcols = slice(h * RB_DV, (h + 1) * RB_DV)
        vb = z_scr[:, o_vb + h * RB_DV:o_vb + (h + 1) * RB_DV].astype(BF16)
        s = _bdot(qr, kr, _NT) * retw_scr[h]
        sb = jnp.where(first, 0.0, sb_scr[h])
        q_dec = jnp.exp(log_gamma * (pos + 1.0))
        k_dec = jnp.exp(log_gamma * (float(t - 1) - pos))
        o = _bdot(s, vb) + _bdot(qr * q_dec, sb)
        sb_scr[h] = math.exp(log_gamma * t) * sb + _bdot(kr * k_dec, vb, _TN)
        mu = jnp.mean(o, axis=-1, keepdims=True)
        oc = o - mu
        var = jnp.mean(oc * oc, axis=-1, keepdims=True)
        on = oc * lax.rsqrt(var + EPS) * gret_ref[:, vcols]
        og = z_scr[:, o_ogb + h * RB_DV:o_ogb + (h + 1) * RB_DV]
        ob_scr[:, vcols] = (on * _silu(og)).astype(BF16)
        yield

    d = x.shape[1]
    ya = jnp.dot(oa_scr[...], wa_ref[...], preferred_element_type=F32)
    yb = jnp.dot(ob_scr[...], wb_ref[...], preferred_element_type=F32)
    y = _sigmoid(z_scr[:, o_ma:o_ma + d]) * ya + _sigmoid(z_scr[:, o_mb:o_mb + d]) * yb
    o_ref[0] = x + gt1 * _bdot(y, wo_ref[...])


def _mixer_kernel(xp_ref, xm_ref, modp_ref, modm_ref, g1_ref, win_ref, lbl_ref, ghg_ref,
                  gret_ref, wa_ref, wb_ref, wo_ref, cos_ref, sin_ref, o_ref, flag_ref,
                  z0_scr, z1_scr, sa_scr, sb_scr, cum_scr, k_scr, retw_scr, lvl_scr, tri_scr,
                  oa_scr, ob_scr, *, layer, depth, tiles_per_seq, base_block):
    t = xp_ref.shape[1]
    n_levels = t.bit_length() - 1
    g = pl.program_id(0)

    @pl.when(g == 0)
    def _build_constants():
        row = lax.broadcasted_iota(jnp.int32, (t, t), 0)
        col = lax.broadcasted_iota(jnp.int32, (t, t), 1)
        diff = row ^ col
        lvl = jnp.zeros((t, t), jnp.int32)
        for bit in range(n_levels):
            lvl = lvl + (diff >= (1 << bit)).astype(jnp.int32)
        lvl_scr[...] = jnp.where(col > row, -1, lvl)
        tri_scr[...] = (col <= row).astype(BF16)
        dist = jnp.abs(row - col).astype(F32)
        visible = (col // CHUNK) <= (row // CHUNK)
        for h in range(RB_HEADS):
            log_gamma = math.log(1.0 - 2.0 ** (-5.0 - h))
            retw_scr[h] = jnp.where(visible, jnp.exp(log_gamma * dist), 0.0)
        z1_scr[...] = jnp.zeros_like(z1_scr)
        sa_scr[...] = jnp.zeros_like(sa_scr)
        sb_scr[...] = jnp.zeros_like(sb_scr)

    first = lax.rem(jnp.maximum(g - 1, 0), tiles_per_seq) == 0

    def step(z_write, z_read):
        proj = _project(xp_ref, modp_ref, g1_ref, win_ref, z_write)
        mix = _mix(z_read, xm_ref, modm_ref, first, lbl_ref, ghg_ref, gret_ref, wa_ref, wb_ref,
                   wo_ref, cos_ref, sin_ref, o_ref, flag_ref, sa_scr, sb_scr, cum_scr, k_scr,
                   retw_scr, lvl_scr, tri_scr, oa_scr, ob_scr, layer=layer, depth=depth,
                   base_block=base_block)
        live = [proj, mix]
        while live:
            for gen in list(live):
                if next(gen, StopIteration) is StopIteration:
                    live.remove(gen)

    @pl.when(lax.rem(g, 2) == 0)
    def _even():
        step(z0_scr, z1_scr)

    @pl.when(lax.rem(g, 2) == 1)
    def _odd():
        step(z1_scr, z0_scr)


def _const_spec(shape):
    return pl.BlockSpec(shape, lambda g: (0,) * len(shape), pipeline_mode=pl.Buffered(1))


def _mixer(x, mod, g1, w_in, lb_logits, g_hgrn, g_ret, w_a, w_b, w_o, cos_t, sin_t, *, layer,
           base_block):
    bsz, seq, d = x.shape
    depth = mod.shape[0]
    t = MIX_TILE
    n_in = w_in.shape[1]
    tps = seq // t
    n_tiles = bsz * tps
    kern = functools.partial(_mixer_kernel, layer=layer, depth=depth, tiles_per_seq=tps,
                             base_block=base_block)

    def proj_tile(g):
        return jnp.minimum(g, n_tiles - 1)

    def mix_tile(g):
        return jnp.maximum(g - 1, 0)

    return pl.pallas_call(
        kern,
        grid=(n_tiles + 1,),
        in_specs=[
            pl.BlockSpec((1, t, d), lambda g: (proj_tile(g) // tps, proj_tile(g) % tps, 0)),
            pl.BlockSpec((1, t, d), lambda g: (mix_tile(g) // tps, mix_tile(g) % tps, 0)),
            pl.BlockSpec((1, 1, 6, d), lambda g: (layer, proj_tile(g) // tps, 0, 0)),
            pl.BlockSpec((1, 1, 6, d), lambda g: (layer, mix_tile(g) // tps, 0, 0)),
            _const_spec((1, d)),
            _const_spec((d, n_in)),
            _const_spec((depth, HA_K)),
            _const_spec((1, HA_V)),
            _const_spec((1, RB_V)),
            _const_spec((HA_V, d)),
            _const_spec((RB_V, d)),
            _const_spec((d, d)),
            pl.BlockSpec((t, RB_DK), lambda g: (mix_tile(g) % tps, 0)),
            pl.BlockSpec((t, RB_DK), lambda g: (mix_tile(g) % tps, 0)),
        ],
        out_specs=[
            pl.BlockSpec((1, t, d), lambda g: (mix_tile(g) // tps, mix_tile(g) % tps, 0)),
            pl.BlockSpec((1, SUBLANES, HA_DK), lambda g: (g, 0, 0)),
        ],
        out_shape=[
            jax.ShapeDtypeStruct((bsz, seq, d), F32),
            jax.ShapeDtypeStruct((n_tiles + 1, SUBLANES, HA_DK), F32),
        ],
        scratch_shapes=[
            pltpu.VMEM((t, n_in), F32),
            pltpu.VMEM((t, n_in), F32),
            pltpu.VMEM((HA_HEADS, HA_DV, HA_DK), F32),
            pltpu.VMEM((RB_HEADS, RB_DK, RB_DV), F32),
            pltpu.VMEM((t, HA_K), F32),
            pltpu.VMEM((t, HA_K), F32),
            pltpu.VMEM((RB_HEADS, t, t), F32),
            pltpu.VMEM((t, t), jnp.int32),
            pltpu.VMEM((t, t), BF16),
            pltpu.VMEM((t, HA_V), BF16),
            pltpu.VMEM((t, RB_V), BF16),
        ],
        compiler_params=pltpu.CompilerParams(
            dimension_semantics=("arbitrary",), vmem_limit_bytes=VMEM_LIMIT_BYTES),
        name=f"mixer_layer{layer}_block{base_block}",
    )(x, x, mod, mod, g1, w_in, lb_logits, g_hgrn, g_ret, w_a, w_b, w_o, cos_t, sin_t)


def _mixer_layer(x, *args, layer):
    fast, peak = _mixer(x, *args, layer=layer, base_block=FAST_BLOCK)
    safe = jnp.max(peak) <= SAFE_PEAK
    return lax.cond(safe, lambda: fast,
                    lambda: _mixer(x, *args, layer=layer, base_block=1)[0])


def _expert_shift(a, delta):
    return pltpu.roll(a, (-delta) % a.shape[0], 0)


def _route(scores, biased):
    idx = lax.broadcasted_iota(jnp.int32, scores.shape, 0)
    pos = idx & (EXP_PER_GROUP - 1)
    grp = idx // EXP_PER_GROUP
    rank = jnp.zeros(scores.shape, jnp.int32)
    for delta in range(-(EXP_PER_GROUP - 1), EXP_PER_GROUP):
        if delta == 0:
            continue
        other = _expert_shift(biased, delta)
        ahead = (other > biased) | ((other == biased) & (delta < 0))
        in_grp = (pos + delta >= 0) & (pos + delta < EXP_PER_GROUP)
        rank = rank + (ahead & in_grp).astype(jnp.int32)
    top = rank < TOP_K
    kept = jnp.where(top, biased, 0.0)
    grp_score = kept
    for delta in range(-(EXP_PER_GROUP - 1), EXP_PER_GROUP):
        if delta == 0:
            continue
        in_grp = (pos + delta >= 0) & (pos + delta < EXP_PER_GROUP)
        grp_score = grp_score + jnp.where(in_grp, _expert_shift(kept, delta), 0.0)
    grank = jnp.zeros(scores.shape, jnp.int32)
    for dg in range(-(N_GROUPS - 1), N_GROUPS):
        if dg == 0:
            continue
        other = _expert_shift(grp_score, dg * EXP_PER_GROUP)
        ahead = (other > grp_score) | ((other == grp_score) & (dg < 0))
        in_rng = (grp + dg >= 0) & (grp + dg < N_GROUPS)
        grank = grank + (ahead & in_rng).astype(jnp.int32)
    sel = top & (grank == 0)
    w = jnp.where(sel, scores, 0.0)
    return w / jnp.sum(w, axis=0, keepdims=True), sel


def _router_kernel(x_ref, mod_ref, g2_ref, wr_ref, bcol_ref, p_ref, idx_ref, cnt_ref,
                   before_scr, run_scr):
    tm, d = x_ref.shape
    half = d // 2
    i = pl.program_id(0)

    @pl.when(i == 0)
    def _init():
        row = lax.broadcasted_iota(jnp.int32, (LANES, LANES), 0)
        col = lax.broadcasted_iota(jnp.int32, (LANES, LANES), 1)
        before_scr[...] = (row < col).astype(BF16)
        run_scr[...] = jnp.zeros_like(run_scr)

    x = x_ref[...]
    modv = mod_ref[0, 0]
    sh2, sc2 = modv[3:4], modv[4:5]
    ms = jnp.mean(x * x, axis=-1, keepdims=True)
    h = (x * lax.rsqrt(ms + EPS)) * g2_ref[...] * (1.0 + sc2) + sh2
    h_hi = h.astype(BF16)
    h_hi32 = h_hi.astype(F32)
    h_lo = (h - h_hi32).astype(BF16)
    wr = wr_ref[...]
    w_hi = wr.astype(BF16)
    w_lo = (wr - w_hi.astype(F32)).astype(BF16)
    logits = (jnp.dot(h_hi, w_hi, preferred_element_type=F32)
              + jnp.dot(h_lo, w_hi, preferred_element_type=F32)
              + jnp.dot(h_hi, w_lo, preferred_element_type=F32))
    scores = _sigmoid(logits.T[0:N_EXPERTS, :])
    comb, sel = _route(scores, scores + bcol_ref[...])

    bits = pltpu.bitcast(h_hi32, jnp.uint32)
    p_ref[:, 0:half] = (bits[:, 0:half] >> 16) | bits[:, half:d]
    comb_rows = jnp.concatenate([comb, jnp.zeros((LANES - N_EXPERTS, tm), F32)], axis=0).T
    p_ref[:, half:half + LANES] = pltpu.bitcast(comb_rows, jnp.uint32)

    sel_f = sel.astype(F32)
    rows = [sel_f[e:e + 1, :] for e in range(N_EXPERTS)]
    bucket = jnp.zeros((1, tm), F32)
    for g in range(N_GROUPS):
        grp_rows = rows[g * EXP_PER_GROUP:(g + 1) * EXP_PER_GROUP]
        bucket = bucket + float(g * N_PAIRS) * functools.reduce(jnp.maximum, grp_rows)
        for p, (a, b) in enumerate(PAIRS):
            if p:
                bucket = bucket + float(p) * (grp_rows[a] * grp_rows[b])
    bucket = bucket.astype(jnp.int32)
    onehot = (lax.broadcasted_iota(jnp.int32, (BUCKET_ROWS, tm), 0) == bucket).astype(F32)
    chunks = [onehot[:, c * LANES:(c + 1) * LANES] for c in range(tm // LANES)]
    inside = jnp.dot(jnp.concatenate(chunks, axis=0).astype(BF16), before_scr[...],
                     preferred_element_type=F32)
    run = run_scr[...]
    seen = run[:, 0:1]
    ranks = []
    for c, chunk in enumerate(chunks):
        earlier = inside[c * BUCKET_ROWS:(c + 1) * BUCKET_ROWS, :] + seen
        ranks.append(jnp.sum(chunk * earlier, axis=0, keepdims=True))
        seen = seen + jnp.sum(chunk, axis=1, keepdims=True)
    rank = jnp.concatenate(ranks, axis=1)
    run = jnp.broadcast_to(seen, run.shape)
    run_scr[...] = run
    idx_ref[0] = jnp.concatenate(
        [bucket, rank.astype(jnp.int32), jnp.zeros((SUBLANES - 2, tm), jnp.int32)], axis=0)
    cnt_ref[...] = run.astype(jnp.int32)


def _router(x2d, mod, g2, wr_pad, b_col, *, layer, seq):
    n_tok, d = x2d.shape
    tm = ROUTE_TILE
    n_tiles = n_tok // tm
    row_w = d // 2 + LANES
    return pl.pallas_call(
        _router_kernel,
        grid=(n_tiles,),
        in_specs=[
            pl.BlockSpec((tm, d), lambda i: (i, 0)),
            pl.BlockSpec((1, 1, 6, d), lambda i: (layer, (i * tm) // seq, 0, 0)),
            pl.BlockSpec((1, d), lambda i: (0, 0)),
            pl.BlockSpec((d, LANES), lambda i: (0, 0)),
            pl.BlockSpec((N_EXPERTS, 1), lambda i: (0, 0)),
        ],
        out_specs=[
            pl.BlockSpec((tm, row_w), lambda i: (i, 0)),
            pl.BlockSpec((1, SUBLANES, tm), lambda i: (i, 0, 0)),
            pl.BlockSpec((BUCKET_ROWS, LANES), lambda i: (0, 0)),
        ],
        out_shape=[
            jax.ShapeDtypeStruct((n_tok, row_w), jnp.uint32),
            jax.ShapeDtypeStruct((n_tiles, SUBLANES, tm), jnp.int32),
            jax.ShapeDtypeStruct((BUCKET_ROWS, LANES), jnp.int32),
        ],
        scratch_shapes=[
            pltpu.VMEM((LANES, LANES), BF16),
            pltpu.VMEM((BUCKET_ROWS, LANES), F32),
        ],
        compiler_params=pltpu.CompilerParams(
            dimension_semantics=("arbitrary",), vmem_limit_bytes=VMEM_LIMIT_BYTES),
        name=f"moe_router{layer}",
    )(x2d, mod, g2, wr_pad, b_col)


def _dispatch_kernel(dest_ref, p_ref, init_ref, hs_ref, sem):
    del init_ref
    rows = p_ref.shape[0] * SUBLANES
    for r in range(rows):
        pltpu.make_async_copy(p_ref.at[r // SUBLANES, pl.ds(r % SUBLANES, 1), :],
                              hs_ref.at[pl.ds(dest_ref[0, 0, r], 1), :],
                              sem).start(priority=r % DMA_PRIORITIES)
    for _ in range(rows):
        pltpu.make_async_copy(p_ref.at[0, pl.ds(0, 1), :], hs_ref.at[pl.ds(0, 1), :],
                              sem).wait()


def _dispatch(p, dest2d, n_sorted, *, layer):
    n_tok, row_w = p.shape
    td = dest2d.shape[2]
    init = jnp.zeros((n_sorted, row_w), jnp.uint32)
    p = p.reshape(n_tok // SUBLANES, SUBLANES, row_w)
    return pl.pallas_call(
        _dispatch_kernel,
        grid=(n_tok // td,),
        in_specs=[
            pl.BlockSpec((1, 1, td), lambda i: (i, 0, 0), memory_space=pltpu.SMEM),
            pl.BlockSpec((td // SUBLANES, SUBLANES, row_w), lambda i: (i, 0, 0)),
            pl.BlockSpec(memory_space=pl.ANY),
        ],
        out_specs=pl.BlockSpec(memory_space=pl.ANY),
        out_shape=jax.ShapeDtypeStruct((n_sorted, row_w), jnp.uint32),
        scratch_shapes=[pltpu.SemaphoreType.DMA(())],
        input_output_aliases={2: 0},
        compiler_params=pltpu.CompilerParams(
            dimension_semantics=("arbitrary",), vmem_limit_bytes=VMEM_LIMIT_BYTES),
        name=f"moe_dispatch{layer}",
    )(dest2d, p, init)


def _expert_kernel(e1_ref, e2_ref, valid_ref, hs_ref, wg1_ref, wu1_ref, wd1_ref,
                   wg2_ref, wu2_ref, wd2_ref, y_ref, wg_scr, wu_scr, wd_scr):
    j = pl.program_id(0)
    half = hs_ref.shape[1] - LANES
    prev = jnp.maximum(j - 1, 0)
    new_pair = (j == 0) | (e1_ref[j] != e1_ref[prev]) | (e2_ref[j] != e2_ref[prev])

    @pl.when((valid_ref[j] != 0) & new_pair)
    def _cast_weights():
        for slot, (wg_ref, wu_ref, wd_ref) in enumerate(((wg1_ref, wu1_ref, wd1_ref),
                                                         (wg2_ref, wu2_ref, wd2_ref))):
            wg_scr[slot] = wg_ref[0, 0].astype(BF16)
            wu_scr[slot] = wu_ref[0, 0].astype(BF16)
            wd_scr[slot] = wd_ref[0, 0].astype(BF16)

    @pl.when(valid_ref[j] != 0)
    def _compute():
        packed = hs_ref[:, 0:half]
        lo = pltpu.bitcast(packed << 16, F32)
        hi = pltpu.bitcast(packed & jnp.uint32(0xFFFF0000), F32)
        h = jnp.concatenate([lo, hi], axis=1).astype(BF16)
        comb = pltpu.bitcast(hs_ref[:, half:half + LANES], F32)
        lane = lax.broadcasted_iota(jnp.int32, comb.shape, 1)

        def expert(e, slot):
            ce = jnp.sum(jnp.where(lane == e, comb, 0.0), axis=-1, keepdims=True)
            hg = jnp.dot(h, wg_scr[slot], preferred_element_type=F32)
            hu = jnp.dot(h, wu_scr[slot], preferred_element_type=F32)
            return ce * _bdot(_silu(hg) * hu, wd_scr[slot])

        y_ref[...] = expert(e1_ref[j], 0) + expert(e2_ref[j], 1)

    @pl.when(valid_ref[j] == 0)
    def _empty():
        y_ref[...] = jnp.zeros_like(y_ref)


def _experts(hs, e1, e2, valid, w_gate, w_up, w_down, *, layer):
    n_sorted, row_w = hs.shape
    _, _, d, d_exp = w_gate.shape
    ts = SORT_TILE
    first = lambda j, e1, e2, valid: (layer, e1[j], 0, 0)
    second = lambda j, e1, e2, valid: (layer, e2[j], 0, 0)
    grid_spec = pltpu.PrefetchScalarGridSpec(
        num_scalar_prefetch=3,
        grid=(n_sorted // ts,),
        in_specs=[
            pl.BlockSpec((ts, row_w), lambda j, e1, e2, valid: (j, 0)),
            pl.BlockSpec((1, 1, d, d_exp), first),
            pl.BlockSpec((1, 1, d, d_exp), first),
            pl.BlockSpec((1, 1, d_exp, d), first),
            pl.BlockSpec((1, 1, d, d_exp), second),
            pl.BlockSpec((1, 1, d, d_exp), second),
            pl.BlockSpec((1, 1, d_exp, d), second),
        ],
        out_specs=pl.BlockSpec((ts, d), lambda j, e1, e2, valid: (j, 0)),
        scratch_shapes=[
            pltpu.VMEM((TOP_K, d, d_exp), BF16),
            pltpu.VMEM((TOP_K, d, d_exp), BF16),
            pltpu.VMEM((TOP_K, d_exp, d), BF16),
        ],
    )
    return pl.pallas_call(
        _expert_kernel,
        grid_spec=grid_spec,
        out_shape=jax.ShapeDtypeStruct((n_sorted, d), F32),
        compiler_params=pltpu.CompilerParams(
            dimension_semantics=("arbitrary",), vmem_limit_bytes=VMEM_LIMIT_BYTES),
        name=f"moe_experts{layer}",
    )(e1, e2, valid, hs, w_gate, w_up, w_down, w_gate, w_up, w_down)


def _combine_kernel(dest_ref, x_ref, mod_ref, gf_ref, ys_ref, o_ref, y_scr, sem, *, final_norm):
    tc, d = x_ref.shape
    for r in range(tc):
        pltpu.make_async_copy(ys_ref.at[pl.ds(dest_ref[0, 0, r], 1), :],
                              y_scr.at[r // SUBLANES, pl.ds(r % SUBLANES, 1), :],
                              sem).start(priority=r % DMA_PRIORITIES)
    for _ in range(tc):
        pltpu.make_async_copy(ys_ref.at[pl.ds(0, 1), :], y_scr.at[0, pl.ds(0, 1), :],
                              sem).wait()

    gt2 = mod_ref[0, 0][5:6]
    out = x_ref[...] + gt2 * y_scr[...].reshape(tc, d)
    if final_norm:
        ms = jnp.mean(out * out, axis=-1, keepdims=True)
        out = out * lax.rsqrt(ms + EPS) * gf_ref[...]
    o_ref[...] = out


def _combine(x2d, mod, g_final, ys, dest2d, *, layer, seq, final_norm):
    n_tok, d = x2d.shape
    tc = dest2d.shape[2]
    kern = functools.partial(_combine_kernel, final_norm=final_norm)
    return pl.pallas_call(
        kern,
        grid=(n_tok // tc,),
        in_specs=[
            pl.BlockSpec((1, 1, tc), lambda i: (i, 0, 0), memory_space=pltpu.SMEM),
            pl.BlockSpec((tc, d), lambda i: (i, 0)),
            pl.BlockSpec((1, 1, 6, d), lambda i: (layer, (i * tc) // seq, 0, 0)),
            pl.BlockSpec((1, d), lambda i: (0, 0)),
            pl.BlockSpec(memory_space=pl.ANY),
        ],
        out_specs=pl.BlockSpec((tc, d), lambda i: (i, 0)),
        out_shape=jax.ShapeDtypeStruct((n_tok, d), F32),
        scratch_shapes=[pltpu.VMEM((tc // SUBLANES, SUBLANES, d), F32),
                        pltpu.SemaphoreType.DMA(())],
        compiler_params=pltpu.CompilerParams(
            dimension_semantics=("arbitrary",), vmem_limit_bytes=VMEM_LIMIT_BYTES),
        name=f"moe_combine{layer}",
    )(dest2d, x2d, mod, g_final, ys)


def _moe(x2d, mod, g2, wr_pad, b_col, w_gate, w_up, w_down, g_final, *, layer, seq, final_norm):
    n_tok, _ = x2d.shape
    p, idx, cnt = _router(x2d, mod, g2, wr_pad, b_col, layer=layer, seq=seq)

    n_buckets = N_GROUPS * N_PAIRS
    counts = cnt[:n_buckets, 0]
    padded = ((counts + SORT_TILE - 1) // SORT_TILE) * SORT_TILE
    ends = jnp.cumsum(padded)
    starts = ends - padded
    dest = starts[idx[:, 0, :].reshape(n_tok)] + idx[:, 1, :].reshape(n_tok)
    n_tiles = n_tok // SORT_TILE + n_buckets
    tile_start = jnp.arange(n_tiles, dtype=jnp.int32) * SORT_TILE
    tile_bucket = jnp.sum((ends[None, :] <= tile_start[:, None]).astype(jnp.int32), axis=1)
    valid = (tile_bucket < n_buckets).astype(jnp.int32)
    tb = jnp.minimum(tile_bucket, n_buckets - 1)
    grp, pair = tb // N_PAIRS, tb % N_PAIRS
    pair_lo = jnp.asarray([a for a, _ in PAIRS], jnp.int32)
    pair_hi = jnp.asarray([b for _, b in PAIRS], jnp.int32)
    e1 = grp * EXP_PER_GROUP + pair_lo[pair]
    e2 = grp * EXP_PER_GROUP + pair_hi[pair]

    hs = _dispatch(p, dest.reshape(n_tok // DISPATCH_TILE, 1, DISPATCH_TILE),
                   n_tiles * SORT_TILE, layer=layer)
    ys = _experts(hs, e1, e2, valid, w_gate, w_up, w_down, layer=layer)
    return _combine(x2d, mod, g_final, ys, dest.reshape(n_tok // COMBINE_TILE, 1, COMBINE_TILE),
                    layer=layer, seq=seq, final_norm=final_norm)


def _rope_tables(seq):
    half = RB_DK // 2
    inv = ROPE_BASE ** (-jnp.arange(half, dtype=F32) / half)
    ang = jnp.arange(seq, dtype=F32)[:, None] * inv[None, :]
    cos, sin = jnp.cos(ang), jnp.sin(ang)
    return jnp.concatenate([cos, cos], axis=-1), jnp.concatenate([-sin, sin], axis=-1)


def kernel(x, c, w_ada, b_ada, g_norm1, g_norm2, w_in, lb_logits, g_hgrn, g_ret, w_branch_a,
           w_branch_b, w_out, w_router, b_router, w_exp_gate, w_exp_up, w_exp_down, g_final):
    bsz, seq, d = x.shape
    depth = w_ada.shape[0]
    assert seq % MIX_TILE == 0 and MIX_TILE % CHUNK == 0
    assert seq % ROUTE_TILE == 0 and seq % COMBINE_TILE == 0
    assert (bsz * seq) % DISPATCH_TILE == 0 and (bsz * seq) % SORT_TILE == 0
    assert w_ada.shape[2] == 6 * d and w_ada.shape[2] % MOD_COLS == 0
    assert N_GROUPS * N_PAIRS <= BUCKET_ROWS and TOP_K == 2

    mod = _modulation(c, w_ada, b_ada).reshape(depth, bsz, 6, d)
    cos_t, sin_t = _rope_tables(seq)
    wr_pad = jnp.pad(w_router, ((0, 0), (0, LANES - N_EXPERTS)))
    b_col = b_router.astype(F32).reshape(N_EXPERTS, 1)
    lbl = lb_logits.astype(F32)

    for l in range(depth):
        x = _mixer_layer(x, mod, g_norm1[l].reshape(1, d), w_in[l].astype(BF16), lbl,
                         g_hgrn[l].reshape(1, HA_V), g_ret[l].reshape(1, RB_V),
                         w_branch_a[l].astype(BF16), w_branch_b[l].astype(BF16),
                         w_out[l].astype(BF16), cos_t, sin_t, layer=l)
        x = _moe(x.reshape(bsz * seq, d), mod, g_norm2[l].reshape(1, d), wr_pad, b_col,
                 w_exp_gate, w_exp_up, w_exp_down, g_final.reshape(1, d),
                 layer=l, seq=seq, final_norm=(l == depth - 1)).reshape(bsz, seq, d)
    return x
```

```python
import functools
import math

import jax
import jax.numpy as jnp
from jax import lax
from jax.experimental import pallas as pl
from jax.experimental.pallas import tpu as pltpu

F32 = jnp.float32
BF16 = jnp.bfloat16

EPS = 1e-6
CHUNK = 64
ROPE_BASE = 10000.0
HA_HEADS, HA_DK, HA_DV = 4, 128, 128
RB_HEADS, RB_DK, RB_DV = 4, 128, 256
HA_K, HA_V = HA_HEADS * HA_DK, HA_HEADS * HA_DV
RB_K, RB_V = RB_HEADS * RB_DK, RB_HEADS * RB_DV
N_EXPERTS, N_GROUPS, TOP_K = 16, 4, 2
EXP_PER_GROUP = N_EXPERTS // N_GROUPS

LANES = 128
SUBLANES = 8
VMEM_LIMIT_BYTES = 56 * 1024 * 1024

MIX_TILE = 256
FAST_BLOCK = 32
SAFE_PEAK = 1e37
ROUTE_TILE = 1024
DISPATCH_TILE = 1024
SORT_TILE = 512
COMBINE_TILE = 1024
MOD_COLS = 1536
DMA_PRIORITIES = 2

PAIRS = tuple((a, b) for a in range(EXP_PER_GROUP) for b in range(a + 1, EXP_PER_GROUP))
N_PAIRS = len(PAIRS)
BUCKET_ROWS = 32

_NT = (((1,), (1,)), ((), ()))
_TN = (((0,), (0,)), ((), ()))


def _sigmoid(v):
    return jax.nn.sigmoid(v)


def _silu(v):
    return v * jax.nn.sigmoid(v)


def _bdot(a, b, dims=None):
    a = a.astype(BF16)
    b = b.astype(BF16)
    if dims is None:
        return jnp.dot(a, b, preferred_element_type=F32)
    return lax.dot_general(a, b, dims, preferred_element_type=F32)


def _mod_kernel(c_ref, w_ref, b_ref, o_ref):
    o_ref[0] = _bdot(_silu(c_ref[...]), w_ref[0]) + b_ref[0]


def _modulation(c, w_ada, b_ada):
    depth, d, six_d = w_ada.shape
    bsz = c.shape[0]
    return pl.pallas_call(
        _mod_kernel,
        grid=(depth, six_d // MOD_COLS),
        in_specs=[
            pl.BlockSpec((bsz, d), lambda l, j: (0, 0)),
            pl.BlockSpec((1, d, MOD_COLS), lambda l, j: (l, 0, j)),
            pl.BlockSpec((1, 1, MOD_COLS), lambda l, j: (l, 0, j)),
        ],
        out_specs=pl.BlockSpec((1, bsz, MOD_COLS), lambda l, j: (l, 0, j)),
        out_shape=jax.ShapeDtypeStruct((depth, bsz, six_d), F32),
        compiler_params=pltpu.CompilerParams(
            dimension_semantics=("arbitrary", "arbitrary"),
            vmem_limit_bytes=VMEM_LIMIT_BYTES),
        name="adaln_modulation",
    )(c, w_ada, b_ada.reshape(depth, 1, six_d))


def _level_ref(cum_scr, cumh, cols, block):
    t = cumh.shape[0]
    half = block // 2
    if block >= SUBLANES:
        pieces = []
        for m in range(t // block):
            r = m * block + half - 1
            pieces.append(jnp.broadcast_to(cum_scr[r:r + 1, cols], (block, cumh.shape[1])))
        return pieces[0] if len(pieces) == 1 else jnp.concatenate(pieces, axis=0)
    pos = lax.broadcasted_iota(jnp.int32, cumh.shape, 0) & (block - 1)
    ref = cumh
    for off in range(-half, half):
        if off == 0:
            continue
        shifted = pltpu.roll(cumh, (-off) % t, 0)
        ref = jnp.where(pos == half - 1 - off, shifted, ref)
    return ref


def _project(x_ref, mod_ref, g1_ref, win_ref, z_scr):
    x = x_ref[0]
    modv = mod_ref[0, 0]
    sh1, sc1 = modv[0:1], modv[1:2]
    ms = jnp.mean(x * x, axis=-1, keepdims=True)
    hn = (x * lax.rsqrt(ms + EPS)) * g1_ref[...]
    hb = (hn * (1.0 + sc1) + sh1).astype(BF16)
    yield
    n_in = win_ref.shape[1]
    seg = 1024
    for s in range(n_in // seg):
        z_scr[:, s * seg:(s + 1) * seg] = jnp.dot(
            hb, win_ref[:, s * seg:(s + 1) * seg], preferred_element_type=F32)
        yield


def _mix(z_scr, x_ref, mod_ref, first, lbl_ref, ghg_ref, gret_ref, wa_ref, wb_ref, wo_ref,
         cos_ref, sin_ref, o_ref, flag_ref, sa_scr, sb_scr, cum_scr, k_scr, retw_scr, lvl_scr,
         tri_scr, oa_scr, ob_scr, *, layer, depth, base_block):
    t = x_ref.shape[1]
    n_levels = t.bit_length() - 1
    x = x_ref[0]
    gt1 = mod_ref[0, 0][2:3]

    rows = [lbl_ref[i:i + 1, :] for i in range(depth)]
    mx = functools.reduce(jnp.maximum, rows)
    ex = [jnp.exp(r - mx) for r in rows]
    tot = functools.reduce(lambda a, b: a + b, ex)
    lb = functools.reduce(lambda a, b: a + b, [e / tot for e in ex[:layer + 1]]) - ex[0] / tot

    o_q, o_f, o_i, o_og = 0, HA_K, 2 * HA_K, 2 * HA_K + HA_V
    o_qb = o_og + HA_V
    o_kb = o_qb + RB_K
    o_vb = o_kb + RB_K
    o_ogb = o_vb + RB_V
    o_ma = o_ogb + RB_V
    o_mb = o_ma + x.shape[1]

    f = lb + (1.0 - lb) * _sigmoid(z_scr[:, o_f:o_f + HA_K])
    k_scr[...] = 1.0 - f
    g = jnp.log(f)
    g_hi = g.astype(BF16)
    g_lo = (g - g_hi.astype(F32)).astype(BF16)
    tri = tri_scr[...]
    cum_scr[...] = (jnp.dot(tri, g_hi, preferred_element_type=F32)
                    + jnp.dot(tri, g_lo, preferred_element_type=F32))

    yield

    lvl = lvl_scr[...]
    base_level = base_block.bit_length() - 1
    peak = jnp.zeros((1, HA_DK), F32)
    for h in range(HA_HEADS):
        cols = slice(h * HA_DK, (h + 1) * HA_DK)
        q = _silu(z_scr[:, o_q + h * HA_DK:o_q + (h + 1) * HA_DK])
        k = k_scr[:, cols]
        v = z_scr[:, o_i + h * HA_DV:o_i + (h + 1) * HA_DV].astype(BF16)
        cumh = cum_scr[:, cols]
        scores = jnp.zeros((t, t), F32)
        for level in range(base_level, n_levels + 1):
            if level > base_level:
                ref = _level_ref(cum_scr, cumh, cols, 1 << level)
                e = jnp.exp(-jnp.abs(cumh - ref))
                qs, ks = q * e, k * e
                mask = lvl == level
            elif base_block == 1:
                qs, ks = q, k
                mask = lvl == 0
            else:
                shift = cumh - _level_ref(cum_scr, cumh, cols, base_block)
                qs, ks = q * jnp.exp(shift), k * jnp.exp(-shift)
                big = jnp.maximum(jnp.abs(qs), jnp.abs(ks))
                peak = jnp.maximum(peak, jnp.max(big, axis=0, keepdims=True))
                mask = (lvl >= 0) & (lvl <= base_level)
            scores = jnp.where(mask, _bdot(qs, ks, _NT), scores)
        st = jnp.where(first, 0.0, sa_scr[h])
        last = cum_scr[t - 1:t, cols]
        o = _bdot(scores, v) + _bdot(q * jnp.exp(cumh), st, _NT)
        sa_scr[h] = st * jnp.exp(last) + _bdot(v, k * jnp.exp(last - cumh), _TN)
        on = o * lax.rsqrt(jnp.mean(o * o, axis=-1, keepdims=True) + EPS) * ghg_ref[:, cols]
        og = z_scr[:, o_og + h * HA_DV:o_og + (h + 1) * HA_DV]
        oa_scr[:, cols] = (on * _silu(og)).astype(BF16)
        yield

    flag_ref[0] = jnp.broadcast_to(peak, (SUBLANES, HA_DK))

    cosf = cos_ref[...]
    sins = sin_ref[...]
    pos = lax.broadcasted_iota(jnp.int32, (t, RB_DK), 0).astype(F32)
    for h in range(RB_HEADS):
        log_gamma = math.log(1.0 - 2.0 ** (-5.0 - h))
        qb = z_scr[:, o_qb + h * RB_DK:o_qb + (h + 1) * RB_DK]
        kb = z_scr[:, o_kb + h * RB_DK:o_kb + (h + 1) * RB_DK]
        qr = (qb * cosf + pltpu.roll(qb, RB_DK // 2, 1) * sins) * (RB_DK ** -0.5)
        kr = kb * cosf + pltpu.roll(kb, RB_DK // 2, 1) * sins
        vcols = slice(h * RB_DV, (h + 1) * RB_DV)
        vb = z_scr[:, o_vb + h * RB_DV:o_vb + (h + 1) * RB_DV].astype(BF16)
        s = _bdot(qr, kr, _NT) * retw_scr[h]
        sb = jnp.where(first, 0.0, sb_scr[h])
        q_dec = jnp.exp(log_gamma * (pos + 1.0))
        k_dec = jnp.exp(log_gamma * (float(t - 1) - pos))
        o = _bdot(s, vb) + _bdot(qr * q_dec, sb)
        sb_scr[h] = math.exp(log_gamma * t) * sb + _bdot(kr * k_dec, vb, _TN)
        mu = jnp.mean(o, axis=-1, keepdims=True)
        oc = o - mu
        var = jnp.mean(oc * oc, axis=-1, keepdims=True)
        on = oc * lax.rsqrt(var + EPS) * gret_ref[:, vcols]
        og = z_scr[:, o_ogb + h * RB_DV:o_ogb + (h + 1) * RB_DV]
        ob_scr[:, vcols] = (on * _silu(og)).astype(BF16)
        yield

    d = x.shape[1]
    ya = jnp.dot(oa_scr[...], wa_ref[...], preferred_element_type=F32)
    yb = jnp.dot(ob_scr[...], wb_ref[...], preferred_element_type=F32)
    y = _sigmoid(z_scr[:, o_ma:o_ma + d]) * ya + _sigmoid(z_scr[:, o_mb:o_mb + d]) * yb
    o_ref[0] = x + gt1 * _bdot(y, wo_ref[...])


def _mixer_kernel(xp_ref, xm_ref, modp_ref, modm_ref, g1_ref, win_ref, lbl_ref, ghg_ref,
                  gret_ref, wa_ref, wb_ref, wo_ref, cos_ref, sin_ref, o_ref, flag_ref,
                  z0_scr, z1_scr, sa_scr, sb_scr, cum_scr, k_scr, retw_scr, lvl_scr, tri_scr,
                  oa_scr, ob_scr, *, layer, depth, tiles_per_seq, base_block):
    t = xp_ref.shape[1]
    n_levels = t.bit_length() - 1
    g = pl.program_id(0)

    @pl.when(g == 0)
    def _build_constants():
        row = lax.broadcasted_iota(jnp.int32, (t, t), 0)
        col = lax.broadcasted_iota(jnp.int32, (t, t), 1)
        diff = row ^ col
        lvl = jnp.zeros((t, t), jnp.int32)
        for bit in range(n_levels):
            lvl = lvl + (diff >= (1 << bit)).astype(jnp.int32)
        lvl_scr[...] = jnp.where(col > row, -1, lvl)
        tri_scr[...] = (col <= row).astype(BF16)
        dist = jnp.abs(row - col).astype(F32)
        visible = (col // CHUNK) <= (row // CHUNK)
        for h in range(RB_HEADS):
            log_gamma = math.log(1.0 - 2.0 ** (-5.0 - h))
            retw_scr[h] = jnp.where(visible, jnp.exp(log_gamma * dist), 0.0)
        z1_scr[...] = jnp.zeros_like(z1_scr)
        sa_scr[...] = jnp.zeros_like(sa_scr)
        sb_scr[...] = jnp.zeros_like(sb_scr)

    first = lax.rem(jnp.maximum(g - 1, 0), tiles_per_seq) == 0

    def step(z_write, z_read):
        proj = _project(xp_ref, modp_ref, g1_ref, win_ref, z_write)
        mix = _mix(z_read, xm_ref, modm_ref, first, lbl_ref, ghg_ref, gret_ref, wa_ref, wb_ref,
                   wo_ref, cos_ref, sin_ref, o_ref, flag_ref, sa_scr, sb_scr, cum_scr, k_scr,
                   retw_scr, lvl_scr, tri_scr, oa_scr, ob_scr, layer=layer, depth=depth,
                   base_block=base_block)
        live = [proj, mix]
        while live:
            for gen in list(live):
                if next(gen, StopIteration) is StopIteration:
                    live.remove(gen)

    @pl.when(lax.rem(g, 2) == 0)
    def _even():
        step(z0_scr, z1_scr)

    @pl.when(lax.rem(g, 2) == 1)
    def _odd():
        step(z1_scr, z0_scr)


def _const_spec(shape):
    return pl.BlockSpec(shape, lambda g: (0,) * len(shape), pipeline_mode=pl.Buffered(1))


def _mixer(x, mod, g1, w_in, lb_logits, g_hgrn, g_ret, w_a, w_b, w_o, cos_t, sin_t, *, layer,
           base_block):
    bsz, seq, d = x.shape
    depth = mod.shape[0]
    t = MIX_TILE
    n_in = w_in.shape[1]
    tps = seq // t
    n_tiles = bsz * tps
    kern = functools.partial(_mixer_kernel, layer=layer, depth=depth, tiles_per_seq=tps,
                             base_block=base_block)

    def proj_tile(g):
        return jnp.minimum(g, n_tiles - 1)

    def mix_tile(g):
        return jnp.maximum(g - 1, 0)

    return pl.pallas_call(
        kern,
        grid=(n_tiles + 1,),
        in_specs=[
            pl.BlockSpec((1, t, d), lambda g: (proj_tile(g) // tps, proj_tile(g) % tps, 0)),
            pl.BlockSpec((1, t, d), lambda g: (mix_tile(g) // tps, mix_tile(g) % tps, 0)),
            pl.BlockSpec((1, 1, 6, d), lambda g: (layer, proj_tile(g) // tps, 0, 0)),
            pl.BlockSpec((1, 1, 6, d), lambda g: (layer, mix_tile(g) // tps, 0, 0)),
            _const_spec((1, d)),
            _const_spec((d, n_in)),
            _const_spec((depth, HA_K)),
            _const_spec((1, HA_V)),
            _const_spec((1, RB_V)),
            _const_spec((HA_V, d)),
            _const_spec((RB_V, d)),
            _const_spec((d, d)),
            pl.BlockSpec((t, RB_DK), lambda g: (mix_tile(g) % tps, 0)),
            pl.BlockSpec((t, RB_DK), lambda g: (mix_tile(g) % tps, 0)),
        ],
        out_specs=[
            pl.BlockSpec((1, t, d), lambda g: (mix_tile(g) // tps, mix_tile(g) % tps, 0)),
            pl.BlockSpec((1, SUBLANES, HA_DK), lambda g: (g, 0, 0)),
        ],
        out_shape=[
            jax.ShapeDtypeStruct((bsz, seq, d), F32),
            jax.ShapeDtypeStruct((n_tiles + 1, SUBLANES, HA_DK), F32),
        ],
        scratch_shapes=[
            pltpu.VMEM((t, n_in), F32),
            pltpu.VMEM((t, n_in), F32),
            pltpu.VMEM((HA_HEADS, HA_DV, HA_DK), F32),
            pltpu.VMEM((RB_HEADS, RB_DK, RB_DV), F32),
            pltpu.VMEM((t, HA_K), F32),
            pltpu.VMEM((t, HA_K), F32),
            pltpu.VMEM((RB_HEADS, t, t), F32),
            pltpu.VMEM((t, t), jnp.int32),
            pltpu.VMEM((t, t), BF16),
            pltpu.VMEM((t, HA_V), BF16),
            pltpu.VMEM((t, RB_V), BF16),
        ],
        compiler_params=pltpu.CompilerParams(
            dimension_semantics=("arbitrary",), vmem_limit_bytes=VMEM_LIMIT_BYTES),
        name=f"mixer_layer{layer}_block{base_block}",
    )(x, x, mod, mod, g1, w_in, lb_logits, g_hgrn, g_ret, w_a, w_b, w_o, cos_t, sin_t)


def _mixer_layer(x, *args, layer):
    fast, peak = _mixer(x, *args, layer=layer, base_block=FAST_BLOCK)
    safe = jnp.max(peak) <= SAFE_PEAK
    return lax.cond(safe, lambda: fast,
                    lambda: _mixer(x, *args, layer=layer, base_block=1)[0])


def _expert_shift(a, delta):
    return pltpu.roll(a, (-delta) % a.shape[0], 0)


def _route(scores, biased):
    idx = lax.broadcasted_iota(jnp.int32, scores.shape, 0)
    pos = idx & (EXP_PER_GROUP - 1)
    grp = idx // EXP_PER_GROUP
    rank = jnp.zeros(scores.shape, jnp.int32)
    for delta in range(-(EXP_PER_GROUP - 1), EXP_PER_GROUP):
        if delta == 0:
            continue
        other = _expert_shift(biased, delta)
        ahead = (other > biased) | ((other == biased) & (delta < 0))
        in_grp = (pos + delta >= 0) & (pos + delta < EXP_PER_GROUP)
        rank = rank + (ahead & in_grp).astype(jnp.int32)
    top = rank < TOP_K
    kept = jnp.where(top, biased, 0.0)
    grp_score = kept
    for delta in range(-(EXP_PER_GROUP - 1), EXP_PER_GROUP):
        if delta == 0:
            continue
        in_grp = (pos + delta >= 0) & (pos + delta < EXP_PER_GROUP)
        grp_score = grp_score + jnp.where(in_grp, _expert_shift(kept, delta), 0.0)
    grank = jnp.zeros(scores.shape, jnp.int32)
    for dg in range(-(N_GROUPS - 1), N_GROUPS):
        if dg == 0:
            continue
        other = _expert_shift(grp_score, dg * EXP_PER_GROUP)
        ahead = (other > grp_score) | ((other == grp_score) & (dg < 0))
        in_rng = (grp + dg >= 0) & (grp + dg < N_GROUPS)
        grank = grank + (ahead & in_rng).astype(jnp.int32)
    sel = top & (grank == 0)
    w = jnp.where(sel, scores, 0.0)
    return w / jnp.sum(w, axis=0, keepdims=True), sel


def _router_kernel(x_ref, mod_ref, g2_ref, wr_ref, bcol_ref, p_ref, idx_ref, cnt_ref,
                   before_scr, run_scr):
    tm, d = x_ref.shape
    half = d // 2
    i = pl.program_id(0)

    @pl.when(i == 0)
    def _init():
        row = lax.broadcasted_iota(jnp.int32, (LANES, LANES), 0)
        col = lax.broadcasted_iota(jnp.int32, (LANES, LANES), 1)
        before_scr[...] = (row < col).astype(BF16)
        run_scr[...] = jnp.zeros_like(run_scr)

    x = x_ref[...]
    modv = mod_ref[0, 0]
    sh2, sc2 = modv[3:4], modv[4:5]
    ms = jnp.mean(x * x, axis=-1, keepdims=True)
    h = (x * lax.rsqrt(ms + EPS)) * g2_ref[...] * (1.0 + sc2) + sh2
    h_hi = h.astype(BF16)
    h_hi32 = h_hi.astype(F32)
    h_lo = (h - h_hi32).astype(BF16)
    wr = wr_ref[...]
    w_hi = wr.astype(BF16)
    w_lo = (wr - w_hi.astype(F32)).astype(BF16)
    logits = (jnp.dot(h_hi, w_hi, preferred_element_type=F32)
              + jnp.dot(h_lo, w_hi, preferred_element_type=F32)
              + jnp.dot(h_hi, w_lo, preferred_element_type=F32))
    scores = _sigmoid(logits.T[0:N_EXPERTS, :])
    comb, sel = _route(scores, scores + bcol_ref[...])

    bits = pltpu.bitcast(h_hi32, jnp.uint32)
    packed = (bits[:, 0:half] >> 16) | bits[:, half:d]
    comb_rows = jnp.concatenate([comb, jnp.zeros((LANES - N_EXPERTS, tm), F32)], axis=0).T
    n_packed = half // LANES
    for c in range(SUBLANES):
        if c < n_packed:
            piece = packed[:, c * LANES:(c + 1) * LANES]
        elif c == n_packed:
            piece = pltpu.bitcast(comb_rows, jnp.uint32)
        else:
            piece = jnp.zeros((tm, LANES), jnp.uint32)
        p_ref[pl.ds(c, tm, stride=SUBLANES), :] = piece

    sel_f = sel.astype(F32)
    rows = [sel_f[e:e + 1, :] for e in range(N_EXPERTS)]
    bucket = jnp.zeros((1, tm), F32)
    for g in range(N_GROUPS):
        grp_rows = rows[g * EXP_PER_GROUP:(g + 1) * EXP_PER_GROUP]
        bucket = bucket + float(g * N_PAIRS) * functools.reduce(jnp.maximum, grp_rows)
        for p, (a, b) in enumerate(PAIRS):
            if p:
                bucket = bucket + float(p) * (grp_rows[a] * grp_rows[b])
    bucket = bucket.astype(jnp.int32)
    onehot = (lax.broadcasted_iota(jnp.int32, (BUCKET_ROWS, tm), 0) == bucket).astype(F32)
    chunks = [onehot[:, c * LANES:(c + 1) * LANES] for c in range(tm // LANES)]
    inside = jnp.dot(jnp.concatenate(chunks, axis=0).astype(BF16), before_scr[...],
                     preferred_element_type=F32)
    run = run_scr[...]
    seen = run[:, 0:1]
    ranks = []
    for c, chunk in enumerate(chunks):
        earlier = inside[c * BUCKET_ROWS:(c + 1) * BUCKET_ROWS, :] + seen
        ranks.append(jnp.sum(chunk * earlier, axis=0, keepdims=True))
        seen = seen + jnp.sum(chunk, axis=1, keepdims=True)
    rank = jnp.concatenate(ranks, axis=1)
    run = jnp.broadcast_to(seen, run.shape)
    run_scr[...] = run
    idx_ref[0] = jnp.concatenate(
        [bucket, rank.astype(jnp.int32), jnp.zeros((SUBLANES - 2, tm), jnp.int32)], axis=0)
    cnt_ref[...] = run.astype(jnp.int32)


def _router(x2d, mod, g2, wr_pad, b_col, *, layer, seq):
    n_tok, d = x2d.shape
    tm = ROUTE_TILE
    n_tiles = n_tok // tm
    assert d // 2 + LANES <= SUBLANES * LANES
    return pl.pallas_call(
        _router_kernel,
        grid=(n_tiles,),
        in_specs=[
            pl.BlockSpec((tm, d), lambda i: (i, 0)),
            pl.BlockSpec((1, 1, 6, d), lambda i: (layer, (i * tm) // seq, 0, 0)),
            pl.BlockSpec((1, d), lambda i: (0, 0)),
            pl.BlockSpec((d, LANES), lambda i: (0, 0)),
            pl.BlockSpec((N_EXPERTS, 1), lambda i: (0, 0)),
        ],
        out_specs=[
            pl.BlockSpec((tm * SUBLANES, LANES), lambda i: (i, 0)),
            pl.BlockSpec((1, SUBLANES, tm), lambda i: (i, 0, 0)),
            pl.BlockSpec((BUCKET_ROWS, LANES), lambda i: (0, 0)),
        ],
        out_shape=[
            jax.ShapeDtypeStruct((n_tok * SUBLANES, LANES), jnp.uint32),
            jax.ShapeDtypeStruct((n_tiles, SUBLANES, tm), jnp.int32),
            jax.ShapeDtypeStruct((BUCKET_ROWS, LANES), jnp.int32),
        ],
        scratch_shapes=[
            pltpu.VMEM((LANES, LANES), BF16),
            pltpu.VMEM((BUCKET_ROWS, LANES), F32),
        ],
        compiler_params=pltpu.CompilerParams(
            dimension_semantics=("arbitrary",), vmem_limit_bytes=VMEM_LIMIT_BYTES),
        name=f"moe_router{layer}",
    )(x2d, mod, g2, wr_pad, b_col)


def _dispatch_kernel(dest_ref, p_ref, init_ref, hs_ref, sem):
    del init_ref
    tokens = p_ref.shape[0] // SUBLANES
    for r in range(tokens):
        slot = pl.multiple_of(dest_ref[0, 0, r], SUBLANES)
        pltpu.make_async_copy(p_ref.at[pl.ds(r * SUBLANES, SUBLANES), :],
                              hs_ref.at[pl.ds(slot, SUBLANES), :],
                              sem).start(priority=r % DMA_PRIORITIES)
    for _ in range(tokens):
        pltpu.make_async_copy(p_ref.at[pl.ds(0, SUBLANES), :], hs_ref.at[pl.ds(0, SUBLANES), :],
                              sem).wait()


def _dispatch(p, dest2d, n_sorted, *, layer):
    td = dest2d.shape[2]
    n_tok = p.shape[0] // SUBLANES
    init = jnp.zeros((n_sorted * SUBLANES, LANES), jnp.uint32)
    return pl.pallas_call(
        _dispatch_kernel,
        grid=(n_tok // td,),
        in_specs=[
            pl.BlockSpec((1, 1, td), lambda i: (i, 0, 0), memory_space=pltpu.SMEM),
            pl.BlockSpec((td * SUBLANES, LANES), lambda i: (i, 0)),
            pl.BlockSpec(memory_space=pl.ANY),
        ],
        out_specs=pl.BlockSpec(memory_space=pl.ANY),
        out_shape=jax.ShapeDtypeStruct((n_sorted * SUBLANES, LANES), jnp.uint32),
        scratch_shapes=[pltpu.SemaphoreType.DMA(())],
        input_output_aliases={2: 0},
        compiler_params=pltpu.CompilerParams(
            dimension_semantics=("arbitrary",), vmem_limit_bytes=VMEM_LIMIT_BYTES),
        name=f"moe_dispatch{layer}",
    )(dest2d, p, init)


def _expert_kernel(e1_ref, e2_ref, valid_ref, hs_ref, wg1_ref, wu1_ref, wd1_ref,
                   wg2_ref, wu2_ref, wd2_ref, y_ref, wg_scr, wu_scr, wd_scr):
    j = pl.program_id(0)
    ts = hs_ref.shape[0] // SUBLANES
    n_packed = wg_scr.shape[1] // (2 * LANES)
    prev = jnp.maximum(j - 1, 0)
    new_pair = (j == 0) | (e1_ref[j] != e1_ref[prev]) | (e2_ref[j] != e2_ref[prev])

    @pl.when((valid_ref[j] != 0) & new_pair)
    def _cast_weights():
        for slot, (wg_ref, wu_ref, wd_ref) in enumerate(((wg1_ref, wu1_ref, wd1_ref),
                                                         (wg2_ref, wu2_ref, wd2_ref))):
            wg_scr[slot] = wg_ref[0, 0].astype(BF16)
            wu_scr[slot] = wu_ref[0, 0].astype(BF16)
            wd_scr[slot] = wd_ref[0, 0].astype(BF16)

    @pl.when(valid_ref[j] != 0)
    def _compute():
        pieces = [hs_ref[pl.ds(c, ts, stride=SUBLANES), :] for c in range(n_packed)]
        lo = [pltpu.bitcast(p << 16, F32) for p in pieces]
        hi = [pltpu.bitcast(p & jnp.uint32(0xFFFF0000), F32) for p in pieces]
        h = jnp.concatenate(lo + hi, axis=1).astype(BF16)
        comb = pltpu.bitcast(hs_ref[pl.ds(n_packed, ts, stride=SUBLANES), :], F32)
        lane = lax.broadcasted_iota(jnp.int32, comb.shape, 1)

        def expert(e, slot):
            ce = jnp.sum(jnp.where(lane == e, comb, 0.0), axis=-1, keepdims=True)
            hg = jnp.dot(h, wg_scr[slot], preferred_element_type=F32)
            hu = jnp.dot(h, wu_scr[slot], preferred_element_type=F32)
            return ce * _bdot(_silu(hg) * hu, wd_scr[slot])

        y = expert(e1_ref[j], 0) + expert(e2_ref[j], 1)
        for c in range(y.shape[1] // LANES):
            y_ref[pl.ds(c, ts, stride=SUBLANES), :] = y[:, c * LANES:(c + 1) * LANES]

    @pl.when(valid_ref[j] == 0)
    def _empty():
        y_ref[...] = jnp.zeros_like(y_ref)


def _experts(hs, e1, e2, valid, w_gate, w_up, w_down, *, layer):
    n_sorted = hs.shape[0] // SUBLANES
    _, _, d, d_exp = w_gate.shape
    assert d == SUBLANES * LANES
    ts = SORT_TILE
    first = lambda j, e1, e2, valid: (layer, e1[j], 0, 0)
    second = lambda j, e1, e2, valid: (layer, e2[j], 0, 0)
    grid_spec = pltpu.PrefetchScalarGridSpec(
        num_scalar_prefetch=3,
        grid=(n_sorted // ts,),
        in_specs=[
            pl.BlockSpec((ts * SUBLANES, LANES), lambda j, e1, e2, valid: (j, 0)),
            pl.BlockSpec((1, 1, d, d_exp), first),
            pl.BlockSpec((1, 1, d, d_exp), first),
            pl.BlockSpec((1, 1, d_exp, d), first),
            pl.BlockSpec((1, 1, d, d_exp), second),
            pl.BlockSpec((1, 1, d, d_exp), second),
            pl.BlockSpec((1, 1, d_exp, d), second),
        ],
        out_specs=pl.BlockSpec((ts * SUBLANES, LANES), lambda j, e1, e2, valid: (j, 0)),
        scratch_shapes=[
            pltpu.VMEM((TOP_K, d, d_exp), BF16),
            pltpu.VMEM((TOP_K, d, d_exp), BF16),
            pltpu.VMEM((TOP_K, d_exp, d), BF16),
        ],
    )
    return pl.pallas_call(
        _expert_kernel,
        grid_spec=grid_spec,
        out_shape=jax.ShapeDtypeStruct((n_sorted * SUBLANES, LANES), F32),
        compiler_params=pltpu.CompilerParams(
            dimension_semantics=("arbitrary",), vmem_limit_bytes=VMEM_LIMIT_BYTES),
        name=f"moe_experts{layer}",
    )(e1, e2, valid, hs, w_gate, w_up, w_down, w_gate, w_up, w_down)


def _combine_kernel(dest_ref, x_ref, mod_ref, gf_ref, ys_ref, o_ref, y_scr, sem, *, final_norm):
    tc, d = x_ref.shape
    for r in range(tc):
        slot = pl.multiple_of(dest_ref[0, 0, r], SUBLANES)
        pltpu.make_async_copy(ys_ref.at[pl.ds(slot, SUBLANES), :],
                              y_scr.at[pl.ds(r * SUBLANES, SUBLANES), :],
                              sem).start(priority=r % DMA_PRIORITIES)
    for _ in range(tc):
        pltpu.make_async_copy(ys_ref.at[pl.ds(0, SUBLANES), :], y_scr.at[pl.ds(0, SUBLANES), :],
                              sem).wait()

    gt2 = mod_ref[0, 0][5:6]
    y = jnp.concatenate([y_scr[pl.ds(c, tc, stride=SUBLANES), :] for c in range(d // LANES)],
                        axis=1)
    out = x_ref[...] + gt2 * y
    if final_norm:
        ms = jnp.mean(out * out, axis=-1, keepdims=True)
        out = out * lax.rsqrt(ms + EPS) * gf_ref[...]
    o_ref[...] = out


def _combine(x2d, mod, g_final, ys, dest2d, *, layer, seq, final_norm):
    n_tok, d = x2d.shape
    tc = dest2d.shape[2]
    kern = functools.partial(_combine_kernel, final_norm=final_norm)
    return pl.pallas_call(
        kern,
        grid=(n_tok // tc,),
        in_specs=[
            pl.BlockSpec((1, 1, tc), lambda i: (i, 0, 0), memory_space=pltpu.SMEM),
            pl.BlockSpec((tc, d), lambda i: (i, 0)),
            pl.BlockSpec((1, 1, 6, d), lambda i: (layer, (i * tc) // seq, 0, 0)),
            pl.BlockSpec((1, d), lambda i: (0, 0)),
            pl.BlockSpec(memory_space=pl.ANY),
        ],
        out_specs=pl.BlockSpec((tc, d), lambda i: (i, 0)),
        out_shape=jax.ShapeDtypeStruct((n_tok, d), F32),
        scratch_shapes=[pltpu.VMEM((tc * SUBLANES, LANES), F32), pltpu.SemaphoreType.DMA(())],
        compiler_params=pltpu.CompilerParams(
            dimension_semantics=("arbitrary",), vmem_limit_bytes=VMEM_LIMIT_BYTES),
        name=f"moe_combine{layer}",
    )(dest2d, x2d, mod, g_final, ys)


def _moe(x2d, mod, g2, wr_pad, b_col, w_gate, w_up, w_down, g_final, *, layer, seq, final_norm):
    n_tok, _ = x2d.shape
    p, idx, cnt = _router(x2d, mod, g2, wr_pad, b_col, layer=layer, seq=seq)

    n_buckets = N_GROUPS * N_PAIRS
    counts = cnt[:n_buckets, 0]
    padded = ((counts + SORT_TILE - 1) // SORT_TILE) * SORT_TILE
    ends = jnp.cumsum(padded)
    starts = ends - padded
    dest = starts[idx[:, 0, :].reshape(n_tok)] + idx[:, 1, :].reshape(n_tok)
    n_tiles = n_tok // SORT_TILE + n_buckets
    tile_start = jnp.arange(n_tiles, dtype=jnp.int32) * SORT_TILE
    tile_bucket = jnp.sum((ends[None, :] <= tile_start[:, None]).astype(jnp.int32), axis=1)
    valid = (tile_bucket < n_buckets).astype(jnp.int32)
    tb = jnp.minimum(tile_bucket, n_buckets - 1)
    grp, pair = tb // N_PAIRS, tb % N_PAIRS
    pair_lo = jnp.asarray([a for a, _ in PAIRS], jnp.int32)
    pair_hi = jnp.asarray([b for _, b in PAIRS], jnp.int32)
    e1 = grp * EXP_PER_GROUP + pair_lo[pair]
    e2 = grp * EXP_PER_GROUP + pair_hi[pair]

    tile_row = dest * SUBLANES
    hs = _dispatch(p, tile_row.reshape(n_tok // DISPATCH_TILE, 1, DISPATCH_TILE),
                   n_tiles * SORT_TILE, layer=layer)
    ys = _experts(hs, e1, e2, valid, w_gate, w_up, w_down, layer=layer)
    return _combine(x2d, mod, g_final, ys,
                    tile_row.reshape(n_tok // COMBINE_TILE, 1, COMBINE_TILE),
                    layer=layer, seq=seq, final_norm=final_norm)


def _rope_tables(seq):
    half = RB_DK // 2
    inv = ROPE_BASE ** (-jnp.arange(half, dtype=F32) / half)
    ang = jnp.arange(seq, dtype=F32)[:, None] * inv[None, :]
    cos, sin = jnp.cos(ang), jnp.sin(ang)
    return jnp.concatenate([cos, cos], axis=-1), jnp.concatenate([-sin, sin], axis=-1)


def kernel(x, c, w_ada, b_ada, g_norm1, g_norm2, w_in, lb_logits, g_hgrn, g_ret, w_branch_a,
           w_branch_b, w_out, w_router, b_router, w_exp_gate, w_exp_up, w_exp_down, g_final):
    bsz, seq, d = x.shape
    depth = w_ada.shape[0]
    assert seq % MIX_TILE == 0 and MIX_TILE % CHUNK == 0
    assert seq % ROUTE_TILE == 0 and seq % COMBINE_TILE == 0
    assert (bsz * seq) % DISPATCH_TILE == 0 and (bsz * seq) % SORT_TILE == 0
    assert w_ada.shape[2] == 6 * d and w_ada.shape[2] % MOD_COLS == 0
    assert N_GROUPS * N_PAIRS <= BUCKET_ROWS and TOP_K == 2

    mod = _modulation(c, w_ada, b_ada).reshape(depth, bsz, 6, d)
    cos_t, sin_t = _rope_tables(seq)
    wr_pad = jnp.pad(w_router, ((0, 0), (0, LANES - N_EXPERTS)))
    b_col = b_router.astype(F32).reshape(N_EXPERTS, 1)
    lbl = lb_logits.astype(F32)

    for l in range(depth):
        x = _mixer_layer(x, mod, g_norm1[l].reshape(1, d), w_in[l].astype(BF16), lbl,
                         g_hgrn[l].reshape(1, HA_V), g_ret[l].reshape(1, RB_V),
                         w_branch_a[l].astype(BF16), w_branch_b[l].astype(BF16),
                         w_out[l].astype(BF16), cos_t, sin_t, layer=l)
        x = _moe(x.reshape(bsz * seq, d), mod, g_norm2[l].reshape(1, d), wr_pad, b_col,
                 w_exp_gate, w_exp_up, w_exp_down, g_final.reshape(1, d),
                 layer=l, seq=seq, final_norm=(l == depth - 1)).reshape(bsz, seq, d)
    return x
```

```python
import functools
import math

import jax
import jax.numpy as jnp
from jax import lax
from jax.experimental import pallas as pl
from jax.experimental.pallas import tpu as pltpu

F32 = jnp.float32
BF16 = jnp.bfloat16

EPS = 1e-6
CHUNK = 64
ROPE_BASE = 10000.0
HA_HEADS, HA_DK, HA_DV = 4, 128, 128
RB_HEADS, RB_DK, RB_DV = 4, 128, 256
HA_K, HA_V = HA_HEADS * HA_DK, HA_HEADS * HA_DV
RB_K, RB_V = RB_HEADS * RB_DK, RB_HEADS * RB_DV
N_EXPERTS, N_GROUPS, TOP_K = 16, 4, 2
EXP_PER_GROUP = N_EXPERTS // N_GROUPS

LANES = 128
SUBLANES = 8
VMEM_LIMIT_BYTES = 56 * 1024 * 1024

MIX_TILE = 256
FAST_BLOCK = 32
SAFE_PEAK = 1e37
ROUTE_TILE = 1024
DISPATCH_TILE = 2048
SORT_TILE = 512
COMBINE_TILE = 1024
MOD_COLS = 1536
DMA_PRIORITIES = 2

PAIRS = tuple((a, b) for a in range(EXP_PER_GROUP) for b in range(a + 1, EXP_PER_GROUP))
N_PAIRS = len(PAIRS)
BUCKET_ROWS = 32

_NT = (((1,), (1,)), ((), ()))
_TN = (((0,), (0,)), ((), ()))


def _sigmoid(v):
    return jax.nn.sigmoid(v)


def _silu(v):
    return v * jax.nn.sigmoid(v)


def _bdot(a, b, dims=None):
    a = a.astype(BF16)
    b = b.astype(BF16)
    if dims is None:
        return jnp.dot(a, b, preferred_element_type=F32)
    return lax.dot_general(a, b, dims, preferred_element_type=F32)


def _mod_kernel(c_ref, w_ref, b_ref, o_ref):
    o_ref[0] = _bdot(_silu(c_ref[...]), w_ref[0]) + b_ref[0]


def _modulation(c, w_ada, b_ada):
    depth, d, six_d = w_ada.shape
    bsz = c.shape[0]
    return pl.pallas_call(
        _mod_kernel,
        grid=(depth, six_d // MOD_COLS),
        in_specs=[
            pl.BlockSpec((bsz, d), lambda l, j: (0, 0)),
            pl.BlockSpec((1, d, MOD_COLS), lambda l, j: (l, 0, j)),
            pl.BlockSpec((1, 1, MOD_COLS), lambda l, j: (l, 0, j)),
        ],
        out_specs=pl.BlockSpec((1, bsz, MOD_COLS), lambda l, j: (l, 0, j)),
        out_shape=jax.ShapeDtypeStruct((depth, bsz, six_d), F32),
        compiler_params=pltpu.CompilerParams(
            dimension_semantics=("arbitrary", "arbitrary"),
            vmem_limit_bytes=VMEM_LIMIT_BYTES),
        name="adaln_modulation",
    )(c, w_ada, b_ada.reshape(depth, 1, six_d))


def _level_ref(cum_scr, cumh, cols, block):
    t = cumh.shape[0]
    half = block // 2
    if block >= SUBLANES:
        pieces = []
        for m in range(t // block):
            r = m * block + half - 1
            pieces.append(jnp.broadcast_to(cum_scr[r:r + 1, cols], (block, cumh.shape[1])))
        return pieces[0] if len(pieces) == 1 else jnp.concatenate(pieces, axis=0)
    pos = lax.broadcasted_iota(jnp.int32, cumh.shape, 0) & (block - 1)
    ref = cumh
    for off in range(-half, half):
        if off == 0:
            continue
        shifted = pltpu.roll(cumh, (-off) % t, 0)
        ref = jnp.where(pos == half - 1 - off, shifted, ref)
    return ref


def _project(x_ref, mod_ref, g1_ref, win_ref, z_scr):
    x = x_ref[0]
    modv = mod_ref[0, 0]
    sh1, sc1 = modv[0:1], modv[1:2]
    ms = jnp.mean(x * x, axis=-1, keepdims=True)
    hn = (x * lax.rsqrt(ms + EPS)) * g1_ref[...]
    hb = (hn * (1.0 + sc1) + sh1).astype(BF16)
    yield
    n_in = win_ref.shape[1]
    seg = 1024
    for s in range(n_in // seg):
        z_scr[:, s * seg:(s + 1) * seg] = jnp.dot(
            hb, win_ref[:, s * seg:(s + 1) * seg], preferred_element_type=F32)
        yield


def _mix(z_scr, x_ref, mod_ref, first, lbl_ref, ghg_ref, gret_ref, wa_ref, wb_ref, wo_ref,
         cos_ref, sin_ref, o_ref, flag_ref, sa_scr, sb_scr, cum_scr, k_scr, retw_scr, lvl_scr,
         tri_scr, oa_scr, ob_scr, *, layer, depth, base_block):
    t = x_ref.shape[1]
    n_levels = t.bit_length() - 1
    x = x_ref[0]
    gt1 = mod_ref[0, 0][2:3]

    rows = [lbl_ref[i:i + 1, :] for i in range(depth)]
    mx = functools.reduce(jnp.maximum, rows)
    ex = [jnp.exp(r - mx) for r in rows]
    tot = functools.reduce(lambda a, b: a + b, ex)
    lb = functools.reduce(lambda a, b: a + b, [e / tot for e in ex[:layer + 1]]) - ex[0] / tot

    o_q, o_f, o_i, o_og = 0, HA_K, 2 * HA_K, 2 * HA_K + HA_V
    o_qb = o_og + HA_V
    o_kb = o_qb + RB_K
    o_vb = o_kb + RB_K
    o_ogb = o_vb + RB_V
    o_ma = o_ogb + RB_V
    o_mb = o_ma + x.shape[1]

    f = lb + (1.0 - lb) * _sigmoid(z_scr[:, o_f:o_f + HA_K])
    k_scr[...] = 1.0 - f
    g = jnp.log(f)
    g_hi = g.astype(BF16)
    g_lo = (g - g_hi.astype(F32)).astype(BF16)
    tri = tri_scr[...]
    cum_scr[...] = (jnp.dot(tri, g_hi, preferred_element_type=F32)
                    + jnp.dot(tri, g_lo, preferred_element_type=F32))

    yield

    lvl = lvl_scr[...]
    base_level = base_block.bit_length() - 1
    peak = jnp.zeros((1, HA_DK), F32)
    for h in range(HA_HEADS):
        cols = slice(h * HA_DK, (h + 1) * HA_DK)
        q = _silu(z_scr[:, o_q + h * HA_DK:o_q + (h + 1) * HA_DK])
        k = k_scr[:, cols]
        v = z_scr[:, o_i + h * HA_DV:o_i + (h + 1) * HA_DV].astype(BF16)
        cumh = cum_scr[:, cols]
        scores = jnp.zeros((t, t), F32)
        for level in range(base_level, n_levels + 1):
            if level > base_level:
                ref = _level_ref(cum_scr, cumh, cols, 1 << level)
                e = jnp.exp(-jnp.abs(cumh - ref))
                qs, ks = q * e, k * e
                mask = lvl == level
            elif base_block == 1:
                qs, ks = q, k
                mask = lvl == 0
            else:
                shift = cumh - _level_ref(cum_scr, cumh, cols, base_block)
                qs, ks = q * jnp.exp(shift), k * jnp.exp(-shift)
                big = jnp.maximum(jnp.abs(qs), jnp.abs(ks))
                peak = jnp.maximum(peak, jnp.max(big, axis=0, keepdims=True))
                mask = (lvl >= 0) & (lvl <= base_level)
            scores = jnp.where(mask, _bdot(qs, ks, _NT), scores)
        st = jnp.where(first, 0.0, sa_scr[h])
        last = cum_scr[t - 1:t, cols]
        o = _bdot(scores, v) + _bdot(q * jnp.exp(cumh), st, _NT)
        sa_scr[h] = st * jnp.exp(last) + _bdot(v, k * jnp.exp(last - cumh), _TN)
        on = o * lax.rsqrt(jnp.mean(o * o, axis=-1, keepdims=True) + EPS) * ghg_ref[:, cols]
        og = z_scr[:, o_og + h * HA_DV:o_og + (h + 1) * HA_DV]
        oa_scr[:, cols] = (on * _silu(og)).astype(BF16)
        yield

    flag_ref[0] = jnp.broadcast_to(peak, (SUBLANES, HA_DK))

    cosf = cos_ref[...]
    sins = sin_ref[...]
    pos = lax.broadcasted_iota(jnp.int32, (t, RB_DK), 0).astype(F32)
    for h in range(RB_HEADS):
        log_gamma = math.log(1.0 - 2.0 ** (-5.0 - h))
        qb = z_scr[:, o_qb + h * RB_DK:o_qb + (h + 1) * RB_DK]
        kb = z_scr[:, o_kb + h * RB_DK:o_kb + (h + 1) * RB_DK]
        qr = (qb * cosf + pltpu.roll(qb, RB_DK // 2, 1) * sins) * (RB_DK ** -0.5)
        kr = kb * cosf + pltpu.roll(kb, RB_DK // 2, 1) * sins
        vcols = slice(h * RB_DV, (h + 1) * RB_DV)
        vb = z_scr[:, o_vb + h * RB_DV:o_vb + (h + 1) * RB_DV].astype(BF16)
        s = _bdot(qr, kr, _NT) * retw_scr[h]
        sb = jnp.where(first, 0.0, sb_scr[h])
        q_dec = jnp.exp(log_gamma * (pos + 1.0))
        k_dec = jnp.exp(log_gamma * (float(t - 1) - pos))
        o = _bdot(s, vb) + _bdot(qr * q_dec, sb)
        sb_scr[h] = math.exp(log_gamma * t) * sb + _bdot(kr * k_dec, vb, _TN)
        mu = jnp.mean(o, axis=-1, keepdims=True)
        oc = o - mu
        var = jnp.mean(oc * oc, axis=-1, keepdims=True)
        on = oc * lax.rsqrt(var + EPS) * gret_ref[:, vcols]
        og = z_scr[:, o_ogb + h * RB_DV:o_ogb + (h + 1) * RB_DV]
        ob_scr[:, vcols] = (on * _silu(og)).astype(BF16)
        yield

    d = x.shape[1]
    ya = jnp.dot(oa_scr[...], wa_ref[...], preferred_element_type=F32)
    yb = jnp.dot(ob_scr[...], wb_ref[...], preferred_element_type=F32)
    y = _sigmoid(z_scr[:, o_ma:o_ma + d]) * ya + _sigmoid(z_scr[:, o_mb:o_mb + d]) * yb
    o_ref[0] = x + gt1 * _bdot(y, wo_ref[...])


def _mixer_kernel(xp_ref, xm_ref, modp_ref, modm_ref, g1_ref, win_ref, lbl_ref, ghg_ref,
                  gret_ref, wa_ref, wb_ref, wo_ref, cos_ref, sin_ref, o_ref, flag_ref,
                  z0_scr, z1_scr, sa_scr, sb_scr, cum_scr, k_scr, retw_scr, lvl_scr, tri_scr,
                  oa_scr, ob_scr, *, layer, depth, tiles_per_seq, base_block):
    t = xp_ref.shape[1]
    n_levels = t.bit_length() - 1
    g = pl.program_id(0)

    @pl.when(g == 0)
    def _build_constants():
        row = lax.broadcasted_iota(jnp.int32, (t, t), 0)
        col = lax.broadcasted_iota(jnp.int32, (t, t), 1)
        diff = row ^ col
        lvl = jnp.zeros((t, t), jnp.int32)
        for bit in range(n_levels):
            lvl = lvl + (diff >= (1 << bit)).astype(jnp.int32)
        lvl_scr[...] = jnp.where(col > row, -1, lvl)
        tri_scr[...] = (col <= row).astype(BF16)
        dist = jnp.abs(row - col).astype(F32)
        visible = (col // CHUNK) <= (row // CHUNK)
        for h in range(RB_HEADS):
            log_gamma = math.log(1.0 - 2.0 ** (-5.0 - h))
            retw_scr[h] = jnp.where(visible, jnp.exp(log_gamma * dist), 0.0)
        z1_scr[...] = jnp.zeros_like(z1_scr)
        sa_scr[...] = jnp.zeros_like(sa_scr)
        sb_scr[...] = jnp.zeros_like(sb_scr)

    first = lax.rem(jnp.maximum(g - 1, 0), tiles_per_seq) == 0

    def step(z_write, z_read):
        proj = _project(xp_ref, modp_ref, g1_ref, win_ref, z_write)
        mix = _mix(z_read, xm_ref, modm_ref, first, lbl_ref, ghg_ref, gret_ref, wa_ref, wb_ref,
                   wo_ref, cos_ref, sin_ref, o_ref, flag_ref, sa_scr, sb_scr, cum_scr, k_scr,
                   retw_scr, lvl_scr, tri_scr, oa_scr, ob_scr, layer=layer, depth=depth,
                   base_block=base_block)
        live = [proj, mix]
        while live:
            for gen in list(live):
                if next(gen, StopIteration) is StopIteration:
                    live.remove(gen)

    @pl.when(lax.rem(g, 2) == 0)
    def _even():
        step(z0_scr, z1_scr)

    @pl.when(lax.rem(g, 2) == 1)
    def _odd():
        step(z1_scr, z0_scr)


def _const_spec(shape):
    return pl.BlockSpec(shape, lambda g: (0,) * len(shape), pipeline_mode=pl.Buffered(1))


def _mixer(x, mod, g1, w_in, lb_logits, g_hgrn, g_ret, w_a, w_b, w_o, cos_t, sin_t, *, layer,
           base_block):
    bsz, seq, d = x.shape
    depth = mod.shape[0]
    t = MIX_TILE
    n_in = w_in.shape[1]
    tps = seq // t
    n_tiles = bsz * tps
    kern = functools.partial(_mixer_kernel, layer=layer, depth=depth, tiles_per_seq=tps,
                             base_block=base_block)

    def proj_tile(g):
        return jnp.minimum(g, n_tiles - 1)

    def mix_tile(g):
        return jnp.maximum(g - 1, 0)

    return pl.pallas_call(
        kern,
        grid=(n_tiles + 1,),
        in_specs=[
            pl.BlockSpec((1, t, d), lambda g: (proj_tile(g) // tps, proj_tile(g) % tps, 0)),
            pl.BlockSpec((1, t, d), lambda g: (mix_tile(g) // tps, mix_tile(g) % tps, 0)),
            pl.BlockSpec((1, 1, 6, d), lambda g: (layer, proj_tile(g) // tps, 0, 0)),
            pl.BlockSpec((1, 1, 6, d), lambda g: (layer, mix_tile(g) // tps, 0, 0)),
            _const_spec((1, d)),
            _const_spec((d, n_in)),
            _const_spec((depth, HA_K)),
            _const_spec((1, HA_V)),
            _const_spec((1, RB_V)),
            _const_spec((HA_V, d)),
            _const_spec((RB_V, d)),
            _const_spec((d, d)),
            pl.BlockSpec((t, RB_DK), lambda g: (mix_tile(g) % tps, 0)),
            pl.BlockSpec((t, RB_DK), lambda g: (mix_tile(g) % tps, 0)),
        ],
        out_specs=[
            pl.BlockSpec((1, t, d), lambda g: (mix_tile(g) // tps, mix_tile(g) % tps, 0)),
            pl.BlockSpec((1, SUBLANES, HA_DK), lambda g: (g, 0, 0)),
        ],
        out_shape=[
            jax.ShapeDtypeStruct((bsz, seq, d), F32),
            jax.ShapeDtypeStruct((n_tiles + 1, SUBLANES, HA_DK), F32),
        ],
        scratch_shapes=[
            pltpu.VMEM((t, n_in), F32),
            pltpu.VMEM((t, n_in), F32),
            pltpu.VMEM((HA_HEADS, HA_DV, HA_DK), F32),
            pltpu.VMEM((RB_HEADS, RB_DK, RB_DV), F32),
            pltpu.VMEM((t, HA_K), F32),
            pltpu.VMEM((t, HA_K), F32),
            pltpu.VMEM((RB_HEADS, t, t), F32),
            pltpu.VMEM((t, t), jnp.int32),
            pltpu.VMEM((t, t), BF16),
            pltpu.VMEM((t, HA_V), BF16),
            pltpu.VMEM((t, RB_V), BF16),
        ],
        compiler_params=pltpu.CompilerParams(
            dimension_semantics=("arbitrary",), vmem_limit_bytes=VMEM_LIMIT_BYTES),
        name=f"mixer_layer{layer}_block{base_block}",
    )(x, x, mod, mod, g1, w_in, lb_logits, g_hgrn, g_ret, w_a, w_b, w_o, cos_t, sin_t)


def _mixer_layer(x, *args, layer):
    fast, peak = _mixer(x, *args, layer=layer, base_block=FAST_BLOCK)
    safe = jnp.max(peak) <= SAFE_PEAK
    return lax.cond(safe, lambda: fast,
                    lambda: _mixer(x, *args, layer=layer, base_block=1)[0])


def _expert_shift(a, delta):
    return pltpu.roll(a, (-delta) % a.shape[0], 0)


def _route(scores, biased):
    idx = lax.broadcasted_iota(jnp.int32, scores.shape, 0)
    pos = idx & (EXP_PER_GROUP - 1)
    grp = idx // EXP_PER_GROUP
    rank = jnp.zeros(scores.shape, jnp.int32)
    for delta in range(-(EXP_PER_GROUP - 1), EXP_PER_GROUP):
        if delta == 0:
            continue
        other = _expert_shift(biased, delta)
        ahead = (other > biased) | ((other == biased) & (delta < 0))
        in_grp = (pos + delta >= 0) & (pos + delta < EXP_PER_GROUP)
        rank = rank + (ahead & in_grp).astype(jnp.int32)
    top = rank < TOP_K
    kept = jnp.where(top, biased, 0.0)
    grp_score = kept
    for delta in range(-(EXP_PER_GROUP - 1), EXP_PER_GROUP):
        if delta == 0:
            continue
        in_grp = (pos + delta >= 0) & (pos + delta < EXP_PER_GROUP)
        grp_score = grp_score + jnp.where(in_grp, _expert_shift(kept, delta), 0.0)
    grank = jnp.zeros(scores.shape, jnp.int32)
    for dg in range(-(N_GROUPS - 1), N_GROUPS):
        if dg == 0:
            continue
        other = _expert_shift(grp_score, dg * EXP_PER_GROUP)
        ahead = (other > grp_score) | ((other == grp_score) & (dg < 0))
        in_rng = (grp + dg >= 0) & (grp + dg < N_GROUPS)
        grank = grank + (ahead & in_rng).astype(jnp.int32)
    sel = top & (grank == 0)
    w = jnp.where(sel, scores, 0.0)
    return w / jnp.sum(w, axis=0, keepdims=True), sel


def _router_kernel(x_ref, mod_ref, g2_ref, wr_ref, bcol_ref, p_ref, idx_ref, cnt_ref,
                   before_scr, run_scr):
    tm, d = x_ref.shape
    half = d // 2
    i = pl.program_id(0)

    @pl.when(i == 0)
    def _init():
        row = lax.broadcasted_iota(jnp.int32, (LANES, LANES), 0)
        col = lax.broadcasted_iota(jnp.int32, (LANES, LANES), 1)
        before_scr[...] = (row < col).astype(BF16)
        run_scr[...] = jnp.zeros_like(run_scr)

    x = x_ref[...]
    modv = mod_ref[0, 0]
    sh2, sc2 = modv[3:4], modv[4:5]
    ms = jnp.mean(x * x, axis=-1, keepdims=True)
    h = (x * lax.rsqrt(ms + EPS)) * g2_ref[...] * (1.0 + sc2) + sh2
    h_hi = h.astype(BF16)
    h_hi32 = h_hi.astype(F32)
    h_lo = (h - h_hi32).astype(BF16)
    wr = wr_ref[...]
    w_hi = wr.astype(BF16)
    w_lo = (wr - w_hi.astype(F32)).astype(BF16)
    logits = (jnp.dot(h_hi, w_hi, preferred_element_type=F32)
              + jnp.dot(h_lo, w_hi, preferred_element_type=F32)
              + jnp.dot(h_hi, w_lo, preferred_element_type=F32))
    scores = _sigmoid(logits.T[0:N_EXPERTS, :])
    comb, sel = _route(scores, scores + bcol_ref[...])

    bits = pltpu.bitcast(h_hi32, jnp.uint32)
    p_ref[:, 0:half] = (bits[:, 0:half] >> 16) | bits[:, half:d]
    comb_rows = jnp.concatenate([comb, jnp.zeros((LANES - N_EXPERTS, tm), F32)], axis=0).T
    p_ref[:, half:half + LANES] = pltpu.bitcast(comb_rows, jnp.uint32)

    sel_f = sel.astype(F32)
    rows = [sel_f[e:e + 1, :] for e in range(N_EXPERTS)]
    bucket = jnp.zeros((1, tm), F32)
    for g in range(N_GROUPS):
        grp_rows = rows[g * EXP_PER_GROUP:(g + 1) * EXP_PER_GROUP]
        bucket = bucket + float(g * N_PAIRS) * functools.reduce(jnp.maximum, grp_rows)
        for p, (a, b) in enumerate(PAIRS):
            if p:
                bucket = bucket + float(p) * (grp_rows[a] * grp_rows[b])
    bucket = bucket.astype(jnp.int32)
    onehot = (lax.broadcasted_iota(jnp.int32, (BUCKET_ROWS, tm), 0) == bucket).astype(F32)
    chunks = [onehot[:, c * LANES:(c + 1) * LANES] for c in range(tm // LANES)]
    inside = jnp.dot(jnp.concatenate(chunks, axis=0).astype(BF16), before_scr[...],
                     preferred_element_type=F32)
    run = run_scr[...]
    seen = run[:, 0:1]
    ranks = []
    for c, chunk in enumerate(chunks):
        earlier = inside[c * BUCKET_ROWS:(c + 1) * BUCKET_ROWS, :] + seen
        ranks.append(jnp.sum(chunk * earlier, axis=0, keepdims=True))
        seen = seen + jnp.sum(chunk, axis=1, keepdims=True)
    rank = jnp.concatenate(ranks, axis=1)
    run = jnp.broadcast_to(seen, run.shape)
    run_scr[...] = run
    idx_ref[0] = jnp.concatenate(
        [bucket, rank.astype(jnp.int32), jnp.zeros((SUBLANES - 2, tm), jnp.int32)], axis=0)
    cnt_ref[...] = run.astype(jnp.int32)


def _router(x2d, mod, g2, wr_pad, b_col, *, layer, seq):
    n_tok, d = x2d.shape
    tm = ROUTE_TILE
    n_tiles = n_tok // tm
    row_w = d // 2 + LANES
    return pl.pallas_call(
        _router_kernel,
        grid=(n_tiles,),
        in_specs=[
            pl.BlockSpec((tm, d), lambda i: (i, 0)),
            pl.BlockSpec((1, 1, 6, d), lambda i: (layer, (i * tm) // seq, 0, 0)),
            pl.BlockSpec((1, d), lambda i: (0, 0)),
            pl.BlockSpec((d, LANES), lambda i: (0, 0)),
            pl.BlockSpec((N_EXPERTS, 1), lambda i: (0, 0)),
        ],
        out_specs=[
            pl.BlockSpec((tm, row_w), lambda i: (i, 0)),
            pl.BlockSpec((1, SUBLANES, tm), lambda i: (i, 0, 0)),
            pl.BlockSpec((BUCKET_ROWS, LANES), lambda i: (0, 0)),
        ],
        out_shape=[
            jax.ShapeDtypeStruct((n_tok, row_w), jnp.uint32),
            jax.ShapeDtypeStruct((n_tiles, SUBLANES, tm), jnp.int32),
            jax.ShapeDtypeStruct((BUCKET_ROWS, LANES), jnp.int32),
        ],
        scratch_shapes=[
            pltpu.VMEM((LANES, LANES), BF16),
            pltpu.VMEM((BUCKET_ROWS, LANES), F32),
        ],
        compiler_params=pltpu.CompilerParams(
            dimension_semantics=("arbitrary",), vmem_limit_bytes=VMEM_LIMIT_BYTES),
        name=f"moe_router{layer}",
    )(x2d, mod, g2, wr_pad, b_col)


def _dispatch_kernel(dest_ref, p_ref, init_ref, hs_ref, sem):
    del init_ref
    rows = p_ref.shape[0] * SUBLANES
    for r in range(rows):
        pltpu.make_async_copy(p_ref.at[r // SUBLANES, pl.ds(r % SUBLANES, 1), :],
                              hs_ref.at[pl.ds(dest_ref[0, 0, r], 1), :],
                              sem).start(priority=r % DMA_PRIORITIES)
    for _ in range(rows):
        pltpu.make_async_copy(p_ref.at[0, pl.ds(0, 1), :], hs_ref.at[pl.ds(0, 1), :],
                              sem).wait()


def _dispatch(p, dest2d, init, *, layer):
    n_tok, row_w = p.shape
    td = dest2d.shape[2]
    p = p.reshape(n_tok // SUBLANES, SUBLANES, row_w)
    return pl.pallas_call(
        _dispatch_kernel,
        grid=(n_tok // td,),
        in_specs=[
            pl.BlockSpec((1, 1, td), lambda i: (i, 0, 0), memory_space=pltpu.SMEM),
            pl.BlockSpec((td // SUBLANES, SUBLANES, row_w), lambda i: (i, 0, 0)),
            pl.BlockSpec(memory_space=pl.ANY),
        ],
        out_specs=pl.BlockSpec(memory_space=pl.ANY),
        out_shape=jax.ShapeDtypeStruct(init.shape, jnp.uint32),
        scratch_shapes=[pltpu.SemaphoreType.DMA(())],
        input_output_aliases={2: 0},
        compiler_params=pltpu.CompilerParams(
            dimension_semantics=("arbitrary",), vmem_limit_bytes=VMEM_LIMIT_BYTES),
        name=f"moe_dispatch{layer}",
    )(dest2d, p, init)


def _expert_kernel(e1_ref, e2_ref, valid_ref, hs_ref, wg1_ref, wu1_ref, wd1_ref,
                   wg2_ref, wu2_ref, wd2_ref, y_ref, wg_scr, wu_scr, wd_scr):
    j = pl.program_id(0)
    half = hs_ref.shape[1] - LANES
    prev = jnp.maximum(j - 1, 0)
    new_pair = (j == 0) | (e1_ref[j] != e1_ref[prev]) | (e2_ref[j] != e2_ref[prev])

    @pl.when((valid_ref[j] != 0) & new_pair)
    def _cast_weights():
        for slot, (wg_ref, wu_ref, wd_ref) in enumerate(((wg1_ref, wu1_ref, wd1_ref),
                                                         (wg2_ref, wu2_ref, wd2_ref))):
            wg_scr[slot] = wg_ref[0, 0].astype(BF16)
            wu_scr[slot] = wu_ref[0, 0].astype(BF16)
            wd_scr[slot] = wd_ref[0, 0].astype(BF16)

    @pl.when(valid_ref[j] != 0)
    def _compute():
        packed = hs_ref[:, 0:half]
        lo = pltpu.bitcast(packed << 16, F32)
        hi = pltpu.bitcast(packed & jnp.uint32(0xFFFF0000), F32)
        h = jnp.concatenate([lo, hi], axis=1).astype(BF16)
        comb = pltpu.bitcast(hs_ref[:, half:half + LANES], F32)
        lane = lax.broadcasted_iota(jnp.int32, comb.shape, 1)

        def expert(e, slot):
            ce = jnp.sum(jnp.where(lane == e, comb, 0.0), axis=-1, keepdims=True)
            hg = jnp.dot(h, wg_scr[slot], preferred_element_type=F32)
            hu = jnp.dot(h, wu_scr[slot], preferred_element_type=F32)
            return ce * _bdot(_silu(hg) * hu, wd_scr[slot])

        y_ref[...] = expert(e1_ref[j], 0) + expert(e2_ref[j], 1)

    @pl.when(valid_ref[j] == 0)
    def _empty():
        y_ref[...] = jnp.zeros_like(y_ref)


def _experts(hs, e1, e2, valid, w_gate, w_up, w_down, *, layer):
    n_sorted, row_w = hs.shape
    _, _, d, d_exp = w_gate.shape
    ts = SORT_TILE
    first = lambda j, e1, e2, valid: (layer, e1[j], 0, 0)
    second = lambda j, e1, e2, valid: (layer, e2[j], 0, 0)
    grid_spec = pltpu.PrefetchScalarGridSpec(
        num_scalar_prefetch=3,
        grid=(n_sorted // ts,),
        in_specs=[
            pl.BlockSpec((ts, row_w), lambda j, e1, e2, valid: (j, 0)),
            pl.BlockSpec((1, 1, d, d_exp), first),
            pl.BlockSpec((1, 1, d, d_exp), first),
            pl.BlockSpec((1, 1, d_exp, d), first),
            pl.BlockSpec((1, 1, d, d_exp), second),
            pl.BlockSpec((1, 1, d, d_exp), second),
            pl.BlockSpec((1, 1, d_exp, d), second),
        ],
        out_specs=pl.BlockSpec((ts, d), lambda j, e1, e2, valid: (j, 0)),
        scratch_shapes=[
            pltpu.VMEM((TOP_K, d, d_exp), BF16),
            pltpu.VMEM((TOP_K, d, d_exp), BF16),
            pltpu.VMEM((TOP_K, d_exp, d), BF16),
        ],
    )
    return pl.pallas_call(
        _expert_kernel,
        grid_spec=grid_spec,
        out_shape=jax.ShapeDtypeStruct((n_sorted, d), F32),
        compiler_params=pltpu.CompilerParams(
            dimension_semantics=("arbitrary",), vmem_limit_bytes=VMEM_LIMIT_BYTES),
        name=f"moe_experts{layer}",
    )(e1, e2, valid, hs, w_gate, w_up, w_down, w_gate, w_up, w_down)


def _combine_kernel(dest_ref, x_ref, mod_ref, gf_ref, ys_ref, o_ref, y_scr, sem, *, final_norm):
    tc, d = x_ref.shape
    for r in range(tc):
        pltpu.make_async_copy(ys_ref.at[pl.ds(dest_ref[0, 0, r], 1), :],
                              y_scr.at[r // SUBLANES, pl.ds(r % SUBLANES, 1), :],
                              sem).start(priority=r % DMA_PRIORITIES)
    for _ in range(tc):
        pltpu.make_async_copy(ys_ref.at[pl.ds(0, 1), :], y_scr.at[0, pl.ds(0, 1), :],
                              sem).wait()

    gt2 = mod_ref[0, 0][5:6]
    out = x_ref[...] + gt2 * y_scr[...].reshape(tc, d)
    if final_norm:
        ms = jnp.mean(out * out, axis=-1, keepdims=True)
        out = out * lax.rsqrt(ms + EPS) * gf_ref[...]
    o_ref[...] = out


def _combine(x2d, mod, g_final, ys, dest2d, *, layer, seq, final_norm):
    n_tok, d = x2d.shape
    tc = dest2d.shape[2]
    kern = functools.partial(_combine_kernel, final_norm=final_norm)
    return pl.pallas_call(
        kern,
        grid=(n_tok // tc,),
        in_specs=[
            pl.BlockSpec((1, 1, tc), lambda i: (i, 0, 0), memory_space=pltpu.SMEM),
            pl.BlockSpec((tc, d), lambda i: (i, 0)),
            pl.BlockSpec((1, 1, 6, d), lambda i: (layer, (i * tc) // seq, 0, 0)),
            pl.BlockSpec((1, d), lambda i: (0, 0)),
            pl.BlockSpec(memory_space=pl.ANY),
        ],
        out_specs=pl.BlockSpec((tc, d), lambda i: (i, 0)),
        out_shape=jax.ShapeDtypeStruct((n_tok, d), F32),
        scratch_shapes=[pltpu.VMEM((tc // SUBLANES, SUBLANES, d), F32),
                        pltpu.SemaphoreType.DMA(())],
        compiler_params=pltpu.CompilerParams(
            dimension_semantics=("arbitrary",), vmem_limit_bytes=VMEM_LIMIT_BYTES),
        name=f"moe_combine{layer}",
    )(dest2d, x2d, mod, g_final, ys)


def _moe(x2d, mod, g2, wr_pad, b_col, w_gate, w_up, w_down, g_final, sorted_buf, *, layer, seq,
         final_norm):
    n_tok, _ = x2d.shape
    p, idx, cnt = _router(x2d, mod, g2, wr_pad, b_col, layer=layer, seq=seq)

    n_buckets = N_GROUPS * N_PAIRS
    counts = cnt[:n_buckets, 0]
    padded = ((counts + SORT_TILE - 1) // SORT_TILE) * SORT_TILE
    ends = jnp.cumsum(padded)
    starts = ends - padded
    dest = starts[idx[:, 0, :].reshape(n_tok)] + idx[:, 1, :].reshape(n_tok)
    n_tiles = sorted_buf.shape[0] // SORT_TILE
    tile_start = jnp.arange(n_tiles, dtype=jnp.int32) * SORT_TILE
    tile_bucket = jnp.sum((ends[None, :] <= tile_start[:, None]).astype(jnp.int32), axis=1)
    valid = (tile_bucket < n_buckets).astype(jnp.int32)
    tb = jnp.minimum(tile_bucket, n_buckets - 1)
    grp, pair = tb // N_PAIRS, tb % N_PAIRS
    pair_lo = jnp.asarray([a for a, _ in PAIRS], jnp.int32)
    pair_hi = jnp.asarray([b for _, b in PAIRS], jnp.int32)
    e1 = grp * EXP_PER_GROUP + pair_lo[pair]
    e2 = grp * EXP_PER_GROUP + pair_hi[pair]

    hs = _dispatch(p, dest.reshape(n_tok // DISPATCH_TILE, 1, DISPATCH_TILE), sorted_buf,
                   layer=layer)
    ys = _experts(hs, e1, e2, valid, w_gate, w_up, w_down, layer=layer)
    out = _combine(x2d, mod, g_final, ys, dest.reshape(n_tok // COMBINE_TILE, 1, COMBINE_TILE),
                   layer=layer, seq=seq, final_norm=final_norm)
    return out, hs


def _rope_tables(seq):
    half = RB_DK // 2
    inv = ROPE_BASE ** (-jnp.arange(half, dtype=F32) / half)
    ang = jnp.arange(seq, dtype=F32)[:, None] * inv[None, :]
    cos, sin = jnp.cos(ang), jnp.sin(ang)
    return jnp.concatenate([cos, cos], axis=-1), jnp.concatenate([-sin, sin], axis=-1)


def kernel(x, c, w_ada, b_ada, g_norm1, g_norm2, w_in, lb_logits, g_hgrn, g_ret, w_branch_a,
           w_branch_b, w_out, w_router, b_router, w_exp_gate, w_exp_up, w_exp_down, g_final):
    bsz, seq, d = x.shape
    depth = w_ada.shape[0]
    n_tok = bsz * seq
    assert seq % MIX_TILE == 0 and MIX_TILE % CHUNK == 0
    assert seq % ROUTE_TILE == 0 and seq % COMBINE_TILE == 0
    assert n_tok % DISPATCH_TILE == 0 and n_tok % SORT_TILE == 0
    assert w_ada.shape[2] == 6 * d and w_ada.shape[2] % MOD_COLS == 0
    assert N_GROUPS * N_PAIRS <= BUCKET_ROWS and TOP_K == 2

    mod = _modulation(c, w_ada, b_ada).reshape(depth, bsz, 6, d)
    cos_t, sin_t = _rope_tables(seq)
    wr_pad = jnp.pad(w_router, ((0, 0), (0, LANES - N_EXPERTS)))
    b_col = b_router.astype(F32).reshape(N_EXPERTS, 1)
    lbl = lb_logits.astype(F32)
    sorted_rows = n_tok + N_GROUPS * N_PAIRS * SORT_TILE
    sorted_buf = jnp.zeros((sorted_rows, d // 2 + LANES), jnp.uint32)

    for l in range(depth):
        x = _mixer_layer(x, mod, g_norm1[l].reshape(1, d), w_in[l].astype(BF16), lbl,
                         g_hgrn[l].reshape(1, HA_V), g_ret[l].reshape(1, RB_V),
                         w_branch_a[l].astype(BF16), w_branch_b[l].astype(BF16),
                         w_out[l].astype(BF16), cos_t, sin_t, layer=l)
        x, sorted_buf = _moe(x.reshape(n_tok, d), mod, g_norm2[l].reshape(1, d), wr_pad, b_col,
                             w_exp_gate, w_exp_up, w_exp_down, g_final.reshape(1, d), sorted_buf,
                             layer=l, seq=seq, final_norm=(l == depth - 1))
        x = x.reshape(bsz, seq, d)
    return x
```

```python
import functools
import math

import jax
import jax.numpy as jnp
from jax import lax
from jax.experimental import pallas as pl
from jax.experimental.pallas import tpu as pltpu

F32 = jnp.float32
BF16 = jnp.bfloat16

EPS = 1e-6
CHUNK = 64
ROPE_BASE = 10000.0
HA_HEADS, HA_DK, HA_DV = 4, 128, 128
RB_HEADS, RB_DK, RB_DV = 4, 128, 256
HA_K, HA_V = HA_HEADS * HA_DK, HA_HEADS * HA_DV
RB_K, RB_V = RB_HEADS * RB_DK, RB_HEADS * RB_DV
N_EXPERTS, N_GROUPS, TOP_K = 16, 4, 2
EXP_PER_GROUP = N_EXPERTS // N_GROUPS

LANES = 128
SUBLANES = 8
VMEM_LIMIT_BYTES = 56 * 1024 * 1024

MIX_TILE = 256
FAST_BLOCK = 64
SAFE_PEAK = 1e37
ROUTE_TILE = 1024
DISPATCH_TILE = 2048
SORT_TILE = 512
COMBINE_TILE = 1024
MOD_COLS = 1536
DMA_PRIORITIES = 2

PAIRS = tuple((a, b) for a in range(EXP_PER_GROUP) for b in range(a + 1, EXP_PER_GROUP))
N_PAIRS = len(PAIRS)
BUCKET_ROWS = 32

_NT = (((1,), (1,)), ((), ()))
_TN = (((0,), (0,)), ((), ()))


def _sigmoid(v):
    return jax.nn.sigmoid(v)


def _silu(v):
    return v * jax.nn.sigmoid(v)


def _bdot(a, b, dims=None):
    a = a.astype(BF16)
    b = b.astype(BF16)
    if dims is None:
        return jnp.dot(a, b, preferred_element_type=F32)
    return lax.dot_general(a, b, dims, preferred_element_type=F32)


def _mod_kernel(c_ref, w_ref, b_ref, o_ref):
    o_ref[0] = _bdot(_silu(c_ref[...]), w_ref[0]) + b_ref[0]


def _modulation(c, w_ada, b_ada):
    depth, d, six_d = w_ada.shape
    bsz = c.shape[0]
    return pl.pallas_call(
        _mod_kernel,
        grid=(depth, six_d // MOD_COLS),
        in_specs=[
            pl.BlockSpec((bsz, d), lambda l, j: (0, 0)),
            pl.BlockSpec((1, d, MOD_COLS), lambda l, j: (l, 0, j)),
            pl.BlockSpec((1, 1, MOD_COLS), lambda l, j: (l, 0, j)),
        ],
        out_specs=pl.BlockSpec((1, bsz, MOD_COLS), lambda l, j: (l, 0, j)),
        out_shape=jax.ShapeDtypeStruct((depth, bsz, six_d), F32),
        compiler_params=pltpu.CompilerParams(
            dimension_semantics=("arbitrary", "arbitrary"),
            vmem_limit_bytes=VMEM_LIMIT_BYTES),
        name="adaln_modulation",
    )(c, w_ada, b_ada.reshape(depth, 1, six_d))


def _level_ref(cum_scr, cumh, cols, block):
    t = cumh.shape[0]
    half = block // 2
    if block >= SUBLANES:
        pieces = []
        for m in range(t // block):
            r = m * block + half - 1
            pieces.append(jnp.broadcast_to(cum_scr[r:r + 1, cols], (block, cumh.shape[1])))
        return pieces[0] if len(pieces) == 1 else jnp.concatenate(pieces, axis=0)
    pos = lax.broadcasted_iota(jnp.int32, cumh.shape, 0) & (block - 1)
    ref = cumh
    for off in range(-half, half):
        if off == 0:
            continue
        shifted = pltpu.roll(cumh, (-off) % t, 0)
        ref = jnp.where(pos == half - 1 - off, shifted, ref)
    return ref


def _project(x_ref, mod_ref, g1_ref, win_ref, z_scr):
    x = x_ref[0]
    modv = mod_ref[0, 0]
    sh1, sc1 = modv[0:1], modv[1:2]
    ms = jnp.mean(x * x, axis=-1, keepdims=True)
    hn = (x * lax.rsqrt(ms + EPS)) * g1_ref[...]
    hb = (hn * (1.0 + sc1) + sh1).astype(BF16)
    yield
    n_in = win_ref.shape[1]
    seg = 1024
    for s in range(n_in // seg):
        z_scr[:, s * seg:(s + 1) * seg] = jnp.dot(
            hb, win_ref[:, s * seg:(s + 1) * seg], preferred_element_type=F32)
        yield


def _mix(z_scr, x_ref, mod_ref, first, lbl_ref, ghg_ref, gret_ref, wa_ref, wb_ref, wo_ref,
         cos_ref, sin_ref, o_ref, flag_ref, sa_scr, sb_scr, cum_scr, k_scr, retw_scr, lvl_scr,
         tri_scr, oa_scr, ob_scr, *, layer, depth, base_block):
    t = x_ref.shape[1]
    n_levels = t.bit_length() - 1
    x = x_ref[0]
    gt1 = mod_ref[0, 0][2:3]

    rows = [lbl_ref[i:i + 1, :] for i in range(depth)]
    mx = functools.reduce(jnp.maximum, rows)
    ex = [jnp.exp(r - mx) for r in rows]
    tot = functools.reduce(lambda a, b: a + b, ex)
    lb = functools.reduce(lambda a, b: a + b, [e / tot for e in ex[:layer + 1]]) - ex[0] / tot

    o_q, o_f, o_i, o_og = 0, HA_K, 2 * HA_K, 2 * HA_K + HA_V
    o_qb = o_og + HA_V
    o_kb = o_qb + RB_K
    o_vb = o_kb + RB_K
    o_ogb = o_vb + RB_V
    o_ma = o_ogb + RB_V
    o_mb = o_ma + x.shape[1]

    f = lb + (1.0 - lb) * _sigmoid(z_scr[:, o_f:o_f + HA_K])
    k_scr[...] = 1.0 - f
    g = jnp.log(f)
    g_hi = g.astype(BF16)
    g_lo = (g - g_hi.astype(F32)).astype(BF16)
    tri = tri_scr[...]
    cum_scr[...] = (jnp.dot(tri, g_hi, preferred_element_type=F32)
                    + jnp.dot(tri, g_lo, preferred_element_type=F32))

    yield

    lvl = lvl_scr[...]
    base_level = base_block.bit_length() - 1
    peak = jnp.zeros((1, HA_DK), F32)
    for h in range(HA_HEADS):
        cols = slice(h * HA_DK, (h + 1) * HA_DK)
        q = _silu(z_scr[:, o_q + h * HA_DK:o_q + (h + 1) * HA_DK])
        k = k_scr[:, cols]
        v = z_scr[:, o_i + h * HA_DV:o_i + (h + 1) * HA_DV].astype(BF16)
        cumh = cum_scr[:, cols]
        scores = jnp.zeros((t, t), F32)
        for level in range(base_level, n_levels + 1):
            if level > base_level:
                ref = _level_ref(cum_scr, cumh, cols, 1 << level)
                e = jnp.exp(-jnp.abs(cumh - ref))
                qs, ks = q * e, k * e
                mask = lvl == level
            elif base_block == 1:
                qs, ks = q, k
                mask = lvl == 0
            else:
                shift = cumh - _level_ref(cum_scr, cumh, cols, base_block)
                qs, ks = q * jnp.exp(shift), k * jnp.exp(-shift)
                big = jnp.maximum(jnp.abs(qs), jnp.abs(ks))
                peak = jnp.maximum(peak, jnp.max(big, axis=0, keepdims=True))
                mask = (lvl >= 0) & (lvl <= base_level)
            scores = jnp.where(mask, _bdot(qs, ks, _NT), scores)
        st = jnp.where(first, 0.0, sa_scr[h])
        last = cum_scr[t - 1:t, cols]
        o = _bdot(scores, v) + _bdot(q * jnp.exp(cumh), st, _NT)
        sa_scr[h] = st * jnp.exp(last) + _bdot(v, k * jnp.exp(last - cumh), _TN)
        on = o * lax.rsqrt(jnp.mean(o * o, axis=-1, keepdims=True) + EPS) * ghg_ref[:, cols]
        og = z_scr[:, o_og + h * HA_DV:o_og + (h + 1) * HA_DV]
        oa_scr[:, cols] = (on * _silu(og)).astype(BF16)
        yield

    flag_ref[0] = jnp.broadcast_to(peak, (SUBLANES, HA_DK))

    cosf = cos_ref[...]
    sins = sin_ref[...]
    pos = lax.broadcasted_iota(jnp.int32, (t, RB_DK), 0).astype(F32)
    for h in range(RB_HEADS):
        log_gamma = math.log(1.0 - 2.0 ** (-5.0 - h))
        qb = z_scr[:, o_qb + h * RB_DK:o_qb + (h + 1) * RB_DK]
        kb = z_scr[:, o_kb + h * RB_DK:o_kb + (h + 1) * RB_DK]
        qr = (qb * cosf + pltpu.roll(qb, RB_DK // 2, 1) * sins) * (RB_DK ** -0.5)
        kr = kb * cosf + pltpu.roll(kb, RB_DK // 2, 1) * sins
        vcols = slice(h * RB_DV, (h + 1) * RB_DV)
        vb = z_scr[:, o_vb + h * RB_DV:o_vb + (h + 1) * RB_DV].astype(BF16)
        s = _bdot(qr, kr, _NT) * retw_scr[h]
        sb = jnp.where(first, 0.0, sb_scr[h])
        q_dec = jnp.exp(log_gamma * (pos + 1.0))
        k_dec = jnp.exp(log_gamma * (float(t - 1) - pos))
        o = _bdot(s, vb) + _bdot(qr * q_dec, sb)
        sb_scr[h] = math.exp(log_gamma * t) * sb + _bdot(kr * k_dec, vb, _TN)
        mu = jnp.mean(o, axis=-1, keepdims=True)
        oc = o - mu
        var = jnp.mean(oc * oc, axis=-1, keepdims=True)
        on = oc * lax.rsqrt(var + EPS) * gret_ref[:, vcols]
        og = z_scr[:, o_ogb + h * RB_DV:o_ogb + (h + 1) * RB_DV]
        ob_scr[:, vcols] = (on * _silu(og)).astype(BF16)
        yield

    d = x.shape[1]
    ya = jnp.dot(oa_scr[...], wa_ref[...], preferred_element_type=F32)
    yb = jnp.dot(ob_scr[...], wb_ref[...], preferred_element_type=F32)
    y = _sigmoid(z_scr[:, o_ma:o_ma + d]) * ya + _sigmoid(z_scr[:, o_mb:o_mb + d]) * yb
    o_ref[0] = x + gt1 * _bdot(y, wo_ref[...])


def _mixer_kernel(xp_ref, xm_ref, modp_ref, modm_ref, g1_ref, win_ref, lbl_ref, ghg_ref,
                  gret_ref, wa_ref, wb_ref, wo_ref, cos_ref, sin_ref, o_ref, flag_ref,
                  z0_scr, z1_scr, sa_scr, sb_scr, cum_scr, k_scr, retw_scr, lvl_scr, tri_scr,
                  oa_scr, ob_scr, *, layer, depth, tiles_per_seq, base_block):
    t = xp_ref.shape[1]
    n_levels = t.bit_length() - 1
    g = pl.program_id(0)

    @pl.when(g == 0)
    def _build_constants():
        row = lax.broadcasted_iota(jnp.int32, (t, t), 0)
        col = lax.broadcasted_iota(jnp.int32, (t, t), 1)
        diff = row ^ col
        lvl = jnp.zeros((t, t), jnp.int32)
        for bit in range(n_levels):
            lvl = lvl + (diff >= (1 << bit)).astype(jnp.int32)
        lvl_scr[...] = jnp.where(col > row, -1, lvl)
        tri_scr[...] = (col <= row).astype(BF16)
        dist = jnp.abs(row - col).astype(F32)
        visible = (col // CHUNK) <= (row // CHUNK)
        for h in range(RB_HEADS):
            log_gamma = math.log(1.0 - 2.0 ** (-5.0 - h))
            retw_scr[h] = jnp.where(visible, jnp.exp(log_gamma * dist), 0.0)
        z1_scr[...] = jnp.zeros_like(z1_scr)
        sa_scr[...] = jnp.zeros_like(sa_scr)
        sb_scr[...] = jnp.zeros_like(sb_scr)

    first = lax.rem(jnp.maximum(g - 1, 0), tiles_per_seq) == 0

    def step(z_write, z_read):
        proj = _project(xp_ref, modp_ref, g1_ref, win_ref, z_write)
        mix = _mix(z_read, xm_ref, modm_ref, first, lbl_ref, ghg_ref, gret_ref, wa_ref, wb_ref,
                   wo_ref, cos_ref, sin_ref, o_ref, flag_ref, sa_scr, sb_scr, cum_scr, k_scr,
                   retw_scr, lvl_scr, tri_scr, oa_scr, ob_scr, layer=layer, depth=depth,
                   base_block=base_block)
        live = [proj, mix]
        while live:
            for gen in list(live):
                if next(gen, StopIteration) is StopIteration:
                    live.remove(gen)

    @pl.when(lax.rem(g, 2) == 0)
    def _even():
        step(z0_scr, z1_scr)

    @pl.when(lax.rem(g, 2) == 1)
    def _odd():
        step(z1_scr, z0_scr)


def _const_spec(shape):
    return pl.BlockSpec(shape, lambda g: (0,) * len(shape), pipeline_mode=pl.Buffered(1))


def _mixer(x, mod, g1, w_in, lb_logits, g_hgrn, g_ret, w_a, w_b, w_o, cos_t, sin_t, *, layer,
           base_block):
    bsz, seq, d = x.shape
    depth = mod.shape[0]
    t = MIX_TILE
    n_in = w_in.shape[1]
    tps = seq // t
    n_tiles = bsz * tps
    kern = functools.partial(_mixer_kernel, layer=layer, depth=depth, tiles_per_seq=tps,
                             base_block=base_block)

    def proj_tile(g):
        return jnp.minimum(g, n_tiles - 1)

    def mix_tile(g):
        return jnp.maximum(g - 1, 0)

    return pl.pallas_call(
        kern,
        grid=(n_tiles + 1,),
        in_specs=[
            pl.BlockSpec((1, t, d), lambda g: (proj_tile(g) // tps, proj_tile(g) % tps, 0)),
            pl.BlockSpec((1, t, d), lambda g: (mix_tile(g) // tps, mix_tile(g) % tps, 0)),
            pl.BlockSpec((1, 1, 6, d), lambda g: (layer, proj_tile(g) // tps, 0, 0)),
            pl.BlockSpec((1, 1, 6, d), lambda g: (layer, mix_tile(g) // tps, 0, 0)),
            _const_spec((1, d)),
            _const_spec((d, n_in)),
            _const_spec((depth, HA_K)),
            _const_spec((1, HA_V)),
            _const_spec((1, RB_V)),
            _const_spec((HA_V, d)),
            _const_spec((RB_V, d)),
            _const_spec((d, d)),
            pl.BlockSpec((t, RB_DK), lambda g: (mix_tile(g) % tps, 0)),
            pl.BlockSpec((t, RB_DK), lambda g: (mix_tile(g) % tps, 0)),
        ],
        out_specs=[
            pl.BlockSpec((1, t, d), lambda g: (mix_tile(g) // tps, mix_tile(g) % tps, 0)),
            pl.BlockSpec((1, SUBLANES, HA_DK), lambda g: (g, 0, 0)),
        ],
        out_shape=[
            jax.ShapeDtypeStruct((bsz, seq, d), F32),
            jax.ShapeDtypeStruct((n_tiles + 1, SUBLANES, HA_DK), F32),
        ],
        scratch_shapes=[
            pltpu.VMEM((t, n_in), F32),
            pltpu.VMEM((t, n_in), F32),
            pltpu.VMEM((HA_HEADS, HA_DV, HA_DK), F32),
            pltpu.VMEM((RB_HEADS, RB_DK, RB_DV), F32),
            pltpu.VMEM((t, HA_K), F32),
            pltpu.VMEM((t, HA_K), F32),
            pltpu.VMEM((RB_HEADS, t, t), F32),
            pltpu.VMEM((t, t), jnp.int32),
            pltpu.VMEM((t, t), BF16),
            pltpu.VMEM((t, HA_V), BF16),
            pltpu.VMEM((t, RB_V), BF16),
        ],
        compiler_params=pltpu.CompilerParams(
            dimension_semantics=("arbitrary",), vmem_limit_bytes=VMEM_LIMIT_BYTES),
        name=f"mixer_layer{layer}_block{base_block}",
    )(x, x, mod, mod, g1, w_in, lb_logits, g_hgrn, g_ret, w_a, w_b, w_o, cos_t, sin_t)


def _mixer_layer(x, *args, layer):
    fast, peak = _mixer(x, *args, layer=layer, base_block=FAST_BLOCK)
    safe = jnp.max(peak) <= SAFE_PEAK
    return lax.cond(safe, lambda: fast,
                    lambda: _mixer(x, *args, layer=layer, base_block=1)[0])


def _expert_shift(a, delta):
    return pltpu.roll(a, (-delta) % a.shape[0], 0)


def _route(scores, biased):
    idx = lax.broadcasted_iota(jnp.int32, scores.shape, 0)
    pos = idx & (EXP_PER_GROUP - 1)
    grp = idx // EXP_PER_GROUP
    rank = jnp.zeros(scores.shape, jnp.int32)
    for delta in range(-(EXP_PER_GROUP - 1), EXP_PER_GROUP):
        if delta == 0:
            continue
        other = _expert_shift(biased, delta)
        ahead = (other > biased) | ((other == biased) & (delta < 0))
        in_grp = (pos + delta >= 0) & (pos + delta < EXP_PER_GROUP)
        rank = rank + (ahead & in_grp).astype(jnp.int32)
    top = rank < TOP_K
    kept = jnp.where(top, biased, 0.0)
    grp_score = kept
    for delta in range(-(EXP_PER_GROUP - 1), EXP_PER_GROUP):
        if delta == 0:
            continue
        in_grp = (pos + delta >= 0) & (pos + delta < EXP_PER_GROUP)
        grp_score = grp_score + jnp.where(in_grp, _expert_shift(kept, delta), 0.0)
    grank = jnp.zeros(scores.shape, jnp.int32)
    for dg in range(-(N_GROUPS - 1), N_GROUPS):
        if dg == 0:
            continue
        other = _expert_shift(grp_score, dg * EXP_PER_GROUP)
        ahead = (other > grp_score) | ((other == grp_score) & (dg < 0))
        in_rng = (grp + dg >= 0) & (grp + dg < N_GROUPS)
        grank = grank + (ahead & in_rng).astype(jnp.int32)
    sel = top & (grank == 0)
    w = jnp.where(sel, scores, 0.0)
    return w / jnp.sum(w, axis=0, keepdims=True), sel


def _router_kernel(x_ref, mod_ref, g2_ref, wr_ref, bcol_ref, p_ref, idx_ref, cnt_ref,
                   before_scr, run_scr):
    tm, d = x_ref.shape
    half = d // 2
    i = pl.program_id(0)

    @pl.when(i == 0)
    def _init():
        row = lax.broadcasted_iota(jnp.int32, (LANES, LANES), 0)
        col = lax.broadcasted_iota(jnp.int32, (LANES, LANES), 1)
        before_scr[...] = (row < col).astype(BF16)
        run_scr[...] = jnp.zeros_like(run_scr)

    x = x_ref[...]
    modv = mod_ref[0, 0]
    sh2, sc2 = modv[3:4], modv[4:5]
    ms = jnp.mean(x * x, axis=-1, keepdims=True)
    h = (x * lax.rsqrt(ms + EPS)) * g2_ref[...] * (1.0 + sc2) + sh2
    h_hi = h.astype(BF16)
    h_hi32 = h_hi.astype(F32)
    h_lo = (h - h_hi32).astype(BF16)
    wr = wr_ref[...]
    w_hi = wr.astype(BF16)
    w_lo = (wr - w_hi.astype(F32)).astype(BF16)
    logits = (jnp.dot(h_hi, w_hi, preferred_element_type=F32)
              + jnp.dot(h_lo, w_hi, preferred_element_type=F32)
              + jnp.dot(h_hi, w_lo, preferred_element_type=F32))
    scores = _sigmoid(logits.T[0:N_EXPERTS, :])
    comb, sel = _route(scores, scores + bcol_ref[...])

    bits = pltpu.bitcast(h_hi32, jnp.uint32)
    p_ref[:, 0:half] = (bits[:, 0:half] >> 16) | bits[:, half:d]
    comb_rows = jnp.concatenate([comb, jnp.zeros((LANES - N_EXPERTS, tm), F32)], axis=0).T
    p_ref[:, half:half + LANES] = pltpu.bitcast(comb_rows, jnp.uint32)

    sel_f = sel.astype(F32)
    rows = [sel_f[e:e + 1, :] for e in range(N_EXPERTS)]
    bucket = jnp.zeros((1, tm), F32)
    for g in range(N_GROUPS):
        grp_rows = rows[g * EXP_PER_GROUP:(g + 1) * EXP_PER_GROUP]
        bucket = bucket + float(g * N_PAIRS) * functools.reduce(jnp.maximum, grp_rows)
        for p, (a, b) in enumerate(PAIRS):
            if p:
                bucket = bucket + float(p) * (grp_rows[a] * grp_rows[b])
    bucket = bucket.astype(jnp.int32)
    onehot = (lax.broadcasted_iota(jnp.int32, (BUCKET_ROWS, tm), 0) == bucket).astype(F32)
    chunks = [onehot[:, c * LANES:(c + 1) * LANES] for c in range(tm // LANES)]
    inside = jnp.dot(jnp.concatenate(chunks, axis=0).astype(BF16), before_scr[...],
                     preferred_element_type=F32)
    run = run_scr[...]
    seen = run[:, 0:1]
    ranks = []
    for c, chunk in enumerate(chunks):
        earlier = inside[c * BUCKET_ROWS:(c + 1) * BUCKET_ROWS, :] + seen
        ranks.append(jnp.sum(chunk * earlier, axis=0, keepdims=True))
        seen = seen + jnp.sum(chunk, axis=1, keepdims=True)
    rank = jnp.concatenate(ranks, axis=1)
    run = jnp.broadcast_to(seen, run.shape)
    run_scr[...] = run
    idx_ref[0] = jnp.concatenate(
        [bucket, rank.astype(jnp.int32), jnp.zeros((SUBLANES - 2, tm), jnp.int32)], axis=0)
    cnt_ref[...] = run.astype(jnp.int32)


def _router(x2d, mod, g2, wr_pad, b_col, *, layer, seq):
    n_tok, d = x2d.shape
    tm = ROUTE_TILE
    n_tiles = n_tok // tm
    row_w = d // 2 + LANES
    return pl.pallas_call(
        _router_kernel,
        grid=(n_tiles,),
        in_specs=[
            pl.BlockSpec((tm, d), lambda i: (i, 0)),
            pl.BlockSpec((1, 1, 6, d), lambda i: (layer, (i * tm) // seq, 0, 0)),
            pl.BlockSpec((1, d), lambda i: (0, 0)),
            pl.BlockSpec((d, LANES), lambda i: (0, 0)),
            pl.BlockSpec((N_EXPERTS, 1), lambda i: (0, 0)),
        ],
        out_specs=[
            pl.BlockSpec((tm, row_w), lambda i: (i, 0)),
            pl.BlockSpec((1, SUBLANES, tm), lambda i: (i, 0, 0)),
            pl.BlockSpec((BUCKET_ROWS, LANES), lambda i: (0, 0)),
        ],
        out_shape=[
            jax.ShapeDtypeStruct((n_tok, row_w), jnp.uint32),
            jax.ShapeDtypeStruct((n_tiles, SUBLANES, tm), jnp.int32),
            jax.ShapeDtypeStruct((BUCKET_ROWS, LANES), jnp.int32),
        ],
        scratch_shapes=[
            pltpu.VMEM((LANES, LANES), BF16),
            pltpu.VMEM((BUCKET_ROWS, LANES), F32),
        ],
        compiler_params=pltpu.CompilerParams(
            dimension_semantics=("arbitrary",), vmem_limit_bytes=VMEM_LIMIT_BYTES),
        name=f"moe_router{layer}",
    )(x2d, mod, g2, wr_pad, b_col)


def _dispatch_kernel(dest_ref, p_ref, init_ref, hs_ref, sem):
    del init_ref
    rows = p_ref.shape[0] * SUBLANES
    for r in range(rows):
        pltpu.make_async_copy(p_ref.at[r // SUBLANES, pl.ds(r % SUBLANES, 1), :],
                              hs_ref.at[pl.ds(dest_ref[0, 0, r], 1), :],
                              sem).start(priority=r % DMA_PRIORITIES)
    for _ in range(rows):
        pltpu.make_async_copy(p_ref.at[0, pl.ds(0, 1), :], hs_ref.at[pl.ds(0, 1), :],
                              sem).wait()


def _dispatch(p, dest2d, init, *, layer):
    n_tok, row_w = p.shape
    td = dest2d.shape[2]
    p = p.reshape(n_tok // SUBLANES, SUBLANES, row_w)
    return pl.pallas_call(
        _dispatch_kernel,
        grid=(n_tok // td,),
        in_specs=[
            pl.BlockSpec((1, 1, td), lambda i: (i, 0, 0), memory_space=pltpu.SMEM),
            pl.BlockSpec((td // SUBLANES, SUBLANES, row_w), lambda i: (i, 0, 0)),
            pl.BlockSpec(memory_space=pl.ANY),
        ],
        out_specs=pl.BlockSpec(memory_space=pl.ANY),
        out_shape=jax.ShapeDtypeStruct(init.shape, jnp.uint32),
        scratch_shapes=[pltpu.SemaphoreType.DMA(())],
        input_output_aliases={2: 0},
        compiler_params=pltpu.CompilerParams(
            dimension_semantics=("arbitrary",), vmem_limit_bytes=VMEM_LIMIT_BYTES),
        name=f"moe_dispatch{layer}",
    )(dest2d, p, init)


def _expert_kernel(e1_ref, e2_ref, valid_ref, hs_ref, wg1_ref, wu1_ref, wd1_ref,
                   wg2_ref, wu2_ref, wd2_ref, y_ref, wg_scr, wu_scr, wd_scr):
    j = pl.program_id(0)
    half = hs_ref.shape[1] - LANES
    prev = jnp.maximum(j - 1, 0)
    new_pair = (j == 0) | (e1_ref[j] != e1_ref[prev]) | (e2_ref[j] != e2_ref[prev])

    @pl.when((valid_ref[j] != 0) & new_pair)
    def _cast_weights():
        for slot, (wg_ref, wu_ref, wd_ref) in enumerate(((wg1_ref, wu1_ref, wd1_ref),
                                                         (wg2_ref, wu2_ref, wd2_ref))):
            wg_scr[slot] = wg_ref[0, 0].astype(BF16)
            wu_scr[slot] = wu_ref[0, 0].astype(BF16)
            wd_scr[slot] = wd_ref[0, 0].astype(BF16)

    @pl.when(valid_ref[j] != 0)
    def _compute():
        packed = hs_ref[:, 0:half]
        lo = pltpu.bitcast(packed << 16, F32)
        hi = pltpu.bitcast(packed & jnp.uint32(0xFFFF0000), F32)
        h = jnp.concatenate([lo, hi], axis=1).astype(BF16)
        comb = pltpu.bitcast(hs_ref[:, half:half + LANES], F32)
        lane = lax.broadcasted_iota(jnp.int32, comb.shape, 1)

        def expert(e, slot):
            ce = jnp.sum(jnp.where(lane == e, comb, 0.0), axis=-1, keepdims=True)
            hg = jnp.dot(h, wg_scr[slot], preferred_element_type=F32)
            hu = jnp.dot(h, wu_scr[slot], preferred_element_type=F32)
            return ce * _bdot(_silu(hg) * hu, wd_scr[slot])

        y_ref[...] = expert(e1_ref[j], 0) + expert(e2_ref[j], 1)

    @pl.when(valid_ref[j] == 0)
    def _empty():
        y_ref[...] = jnp.zeros_like(y_ref)


def _experts(hs, e1, e2, valid, w_gate, w_up, w_down, *, layer):
    n_sorted, row_w = hs.shape
    _, _, d, d_exp = w_gate.shape
    ts = SORT_TILE
    first = lambda j, e1, e2, valid: (layer, e1[j], 0, 0)
    second = lambda j, e1, e2, valid: (layer, e2[j], 0, 0)
    grid_spec = pltpu.PrefetchScalarGridSpec(
        num_scalar_prefetch=3,
        grid=(n_sorted // ts,),
        in_specs=[
            pl.BlockSpec((ts, row_w), lambda j, e1, e2, valid: (j, 0)),
            pl.BlockSpec((1, 1, d, d_exp), first),
            pl.BlockSpec((1, 1, d, d_exp), first),
            pl.BlockSpec((1, 1, d_exp, d), first),
            pl.BlockSpec((1, 1, d, d_exp), second),
            pl.BlockSpec((1, 1, d, d_exp), second),
            pl.BlockSpec((1, 1, d_exp, d), second),
        ],
        out_specs=pl.BlockSpec((ts, d), lambda j, e1, e2, valid: (j, 0)),
        scratch_shapes=[
            pltpu.VMEM((TOP_K, d, d_exp), BF16),
            pltpu.VMEM((TOP_K, d, d_exp), BF16),
            pltpu.VMEM((TOP_K, d_exp, d), BF16),
        ],
    )
    return pl.pallas_call(
        _expert_kernel,
        grid_spec=grid_spec,
        out_shape=jax.ShapeDtypeStruct((n_sorted, d), F32),
        compiler_params=pltpu.CompilerParams(
            dimension_semantics=("arbitrary",), vmem_limit_bytes=VMEM_LIMIT_BYTES),
        name=f"moe_experts{layer}",
    )(e1, e2, valid, hs, w_gate, w_up, w_down, w_gate, w_up, w_down)


def _combine_kernel(dest_ref, x_ref, mod_ref, gf_ref, ys_ref, o_ref, y_scr, sem, *, final_norm):
    tc, d = x_ref.shape
    for r in range(tc):
        pltpu.make_async_copy(ys_ref.at[pl.ds(dest_ref[0, 0, r], 1), :],
                              y_scr.at[r // SUBLANES, pl.ds(r % SUBLANES, 1), :],
                              sem).start(priority=r % DMA_PRIORITIES)
    for _ in range(tc):
        pltpu.make_async_copy(ys_ref.at[pl.ds(0, 1), :], y_scr.at[0, pl.ds(0, 1), :],
                              sem).wait()

    gt2 = mod_ref[0, 0][5:6]
    out = x_ref[...] + gt2 * y_scr[...].reshape(tc, d)
    if final_norm:
        ms = jnp.mean(out * out, axis=-1, keepdims=True)
        out = out * lax.rsqrt(ms + EPS) * gf_ref[...]
    o_ref[...] = out


def _combine(x2d, mod, g_final, ys, dest2d, *, layer, seq, final_norm):
    n_tok, d = x2d.shape
    tc = dest2d.shape[2]
    kern = functools.partial(_combine_kernel, final_norm=final_norm)
    return pl.pallas_call(
        kern,
        grid=(n_tok // tc,),
        in_specs=[
            pl.BlockSpec((1, 1, tc), lambda i: (i, 0, 0), memory_space=pltpu.SMEM),
            pl.BlockSpec((tc, d), lambda i: (i, 0)),
            pl.BlockSpec((1, 1, 6, d), lambda i: (layer, (i * tc) // seq, 0, 0)),
            pl.BlockSpec((1, d), lambda i: (0, 0)),
            pl.BlockSpec(memory_space=pl.ANY),
        ],
        out_specs=pl.BlockSpec((tc, d), lambda i: (i, 0)),
        out_shape=jax.ShapeDtypeStruct((n_tok, d), F32),
        scratch_shapes=[pltpu.VMEM((tc // SUBLANES, SUBLANES, d), F32),
                        pltpu.SemaphoreType.DMA(())],
        compiler_params=pltpu.CompilerParams(
            dimension_semantics=("arbitrary",), vmem_limit_bytes=VMEM_LIMIT_BYTES),
        name=f"moe_combine{layer}",
    )(dest2d, x2d, mod, g_final, ys)


def _moe(x2d, mod, g2, wr_pad, b_col, w_gate, w_up, w_down, g_final, sorted_buf, *, layer, seq,
         final_norm):
    n_tok, _ = x2d.shape
    p, idx, cnt = _router(x2d, mod, g2, wr_pad, b_col, layer=layer, seq=seq)

    n_buckets = N_GROUPS * N_PAIRS
    counts = cnt[:n_buckets, 0]
    padded = ((counts + SORT_TILE - 1) // SORT_TILE) * SORT_TILE
    ends = jnp.cumsum(padded)
    starts = ends - padded
    dest = starts[idx[:, 0, :].reshape(n_tok)] + idx[:, 1, :].reshape(n_tok)
    n_tiles = sorted_buf.shape[0] // SORT_TILE
    tile_start = jnp.arange(n_tiles, dtype=jnp.int32) * SORT_TILE
    tile_bucket = jnp.sum((ends[None, :] <= tile_start[:, None]).astype(jnp.int32), axis=1)
    valid = (tile_bucket < n_buckets).astype(jnp.int32)
    tb = jnp.minimum(tile_bucket, n_buckets - 1)
    grp, pair = tb // N_PAIRS, tb % N_PAIRS
    pair_lo = jnp.asarray([a for a, _ in PAIRS], jnp.int32)
    pair_hi = jnp.asarray([b for _, b in PAIRS], jnp.int32)
    e1 = grp * EXP_PER_GROUP + pair_lo[pair]
    e2 = grp * EXP_PER_GROUP + pair_hi[pair]

    hs = _dispatch(p, dest.reshape(n_tok // DISPATCH_TILE, 1, DISPATCH_TILE), sorted_buf,
                   layer=layer)
    ys = _experts(hs, e1, e2, valid, w_gate, w_up, w_down, layer=layer)
    out = _combine(x2d, mod, g_final, ys, dest.reshape(n_tok // COMBINE_TILE, 1, COMBINE_TILE),
                   layer=layer, seq=seq, final_norm=final_norm)
    return out, hs


def _rope_tables(seq):
    half = RB_DK // 2
    inv = ROPE_BASE ** (-jnp.arange(half, dtype=F32) / half)
    ang = jnp.arange(seq, dtype=F32)[:, None] * inv[None, :]
    cos, sin = jnp.cos(ang), jnp.sin(ang)
    return jnp.concatenate([cos, cos], axis=-1), jnp.concatenate([-sin, sin], axis=-1)


def kernel(x, c, w_ada, b_ada, g_norm1, g_norm2, w_in, lb_logits, g_hgrn, g_ret, w_branch_a,
           w_branch_b, w_out, w_router, b_router, w_exp_gate, w_exp_up, w_exp_down, g_final):
    bsz, seq, d = x.shape
    depth = w_ada.shape[0]
    n_tok = bsz * seq
    assert seq % MIX_TILE == 0 and MIX_TILE % CHUNK == 0
    assert seq % ROUTE_TILE == 0 and seq % COMBINE_TILE == 0
    assert n_tok % DISPATCH_TILE == 0 and n_tok % SORT_TILE == 0
    assert w_ada.shape[2] == 6 * d and w_ada.shape[2] % MOD_COLS == 0
    assert N_GROUPS * N_PAIRS <= BUCKET_ROWS and TOP_K == 2

    mod = _modulation(c, w_ada, b_ada).reshape(depth, bsz, 6, d)
    cos_t, sin_t = _rope_tables(seq)
    wr_pad = jnp.pad(w_router, ((0, 0), (0, LANES - N_EXPERTS)))
    b_col = b_router.astype(F32).reshape(N_EXPERTS, 1)
    lbl = lb_logits.astype(F32)
    sorted_rows = n_tok + N_GROUPS * N_PAIRS * SORT_TILE
    sorted_buf = jnp.zeros((sorted_rows, d // 2 + LANES), jnp.uint32)

    for l in range(depth):
        x = _mixer_layer(x, mod, g_norm1[l].reshape(1, d), w_in[l].astype(BF16), lbl,
                         g_hgrn[l].reshape(1, HA_V), g_ret[l].reshape(1, RB_V),
                         w_branch_a[l].astype(BF16), w_branch_b[l].astype(BF16),
                         w_out[l].astype(BF16), cos_t, sin_t, layer=l)
        x, sorted_buf = _moe(x.reshape(n_tok, d), mod, g_norm2[l].reshape(1, d), wr_pad, b_col,
                             w_exp_gate, w_exp_up, w_exp_down, g_final.reshape(1, d), sorted_buf,
                             layer=l, seq=seq, final_norm=(l == depth - 1))
        x = x.reshape(bsz, seq, d)
    return x
```

```python
import functools
import math

import jax
import jax.numpy as jnp
from jax import lax
from jax.experimental import pallas as pl
from jax.experimental.pallas import tpu as pltpu

F32 = jnp.float32
BF16 = jnp.bfloat16

EPS = 1e-6
CHUNK = 64
ROPE_BASE = 10000.0
HA_HEADS, HA_DK, HA_DV = 4, 128, 128
RB_HEADS, RB_DK, RB_DV = 4, 128, 256
HA_K, HA_V = HA_HEADS * HA_DK, HA_HEADS * HA_DV
RB_K, RB_V = RB_HEADS * RB_DK, RB_HEADS * RB_DV
N_EXPERTS, N_GROUPS, TOP_K = 16, 4, 2
EXP_PER_GROUP = N_EXPERTS // N_GROUPS

LANES = 128
SUBLANES = 8
VMEM_LIMIT_BYTES = 56 * 1024 * 1024

MIX_TILE = 256
FAST_BLOCK = 64
SAFE_PEAK = 1e37
ROUTE_TILE = 1024
DISPATCH_TILE = 4096
SORT_TILE = 512
COMBINE_TILE = 2048
MOD_COLS = 1536
DMA_PRIORITIES = 2

PAIRS = tuple((a, b) for a in range(EXP_PER_GROUP) for b in range(a + 1, EXP_PER_GROUP))
N_PAIRS = len(PAIRS)
BUCKET_ROWS = 32

_NT = (((1,), (1,)), ((), ()))
_TN = (((0,), (0,)), ((), ()))


def _sigmoid(v):
    return jax.nn.sigmoid(v)


def _silu(v):
    return v * jax.nn.sigmoid(v)


def _bdot(a, b, dims=None):
    a = a.astype(BF16)
    b = b.astype(BF16)
    if dims is None:
        return jnp.dot(a, b, preferred_element_type=F32)
    return lax.dot_general(a, b, dims, preferred_element_type=F32)


def _mod_kernel(c_ref, w_ref, b_ref, o_ref):
    o_ref[0] = _bdot(_silu(c_ref[...]), w_ref[0]) + b_ref[0]


def _modulation(c, w_ada, b_ada):
    depth, d, six_d = w_ada.shape
    bsz = c.shape[0]
    return pl.pallas_call(
        _mod_kernel,
        grid=(depth, six_d // MOD_COLS),
        in_specs=[
            pl.BlockSpec((bsz, d), lambda l, j: (0, 0)),
            pl.BlockSpec((1, d, MOD_COLS), lambda l, j: (l, 0, j)),
            pl.BlockSpec((1, 1, MOD_COLS), lambda l, j: (l, 0, j)),
        ],
        out_specs=pl.BlockSpec((1, bsz, MOD_COLS), lambda l, j: (l, 0, j)),
        out_shape=jax.ShapeDtypeStruct((depth, bsz, six_d), F32),
        compiler_params=pltpu.CompilerParams(
            dimension_semantics=("arbitrary", "arbitrary"),
            vmem_limit_bytes=VMEM_LIMIT_BYTES),
        name="adaln_modulation",
    )(c, w_ada, b_ada.reshape(depth, 1, six_d))


def _level_ref(cum_scr, cumh, cols, block):
    t = cumh.shape[0]
    half = block // 2
    if block >= SUBLANES:
        pieces = []
        for m in range(t // block):
            r = m * block + half - 1
            pieces.append(jnp.broadcast_to(cum_scr[r:r + 1, cols], (block, cumh.shape[1])))
        return pieces[0] if len(pieces) == 1 else jnp.concatenate(pieces, axis=0)
    pos = lax.broadcasted_iota(jnp.int32, cumh.shape, 0) & (block - 1)
    ref = cumh
    for off in range(-half, half):
        if off == 0:
            continue
        shifted = pltpu.roll(cumh, (-off) % t, 0)
        ref = jnp.where(pos == half - 1 - off, shifted, ref)
    return ref


def _project(x_ref, mod_ref, g1_ref, win_ref, z_scr):
    x = x_ref[0]
    modv = mod_ref[0, 0]
    sh1, sc1 = modv[0:1], modv[1:2]
    ms = jnp.mean(x * x, axis=-1, keepdims=True)
    hn = (x * lax.rsqrt(ms + EPS)) * g1_ref[...]
    hb = (hn * (1.0 + sc1) + sh1).astype(BF16)
    yield
    n_in = win_ref.shape[1]
    seg = 1024
    for s in range(n_in // seg):
        z_scr[:, s * seg:(s + 1) * seg] = jnp.dot(
            hb, win_ref[:, s * seg:(s + 1) * seg], preferred_element_type=F32)
        yield


def _mix(z_scr, x_ref, mod_ref, first, lbl_ref, ghg_ref, gret_ref, wa_ref, wb_ref, wo_ref,
         cos_ref, sin_ref, o_ref, flag_ref, sa_scr, sb_scr, cum_scr, k_scr, retw_scr, lvl_scr,
         tri_scr, oa_scr, ob_scr, *, layer, depth, base_block):
    t = x_ref.shape[1]
    n_levels = t.bit_length() - 1
    x = x_ref[0]
    gt1 = mod_ref[0, 0][2:3]

    rows = [lbl_ref[i:i + 1, :] for i in range(depth)]
    mx = functools.reduce(jnp.maximum, rows)
    ex = [jnp.exp(r - mx) for r in rows]
    tot = functools.reduce(lambda a, b: a + b, ex)
    lb = functools.reduce(lambda a, b: a + b, [e / tot for e in ex[:layer + 1]]) - ex[0] / tot

    o_q, o_f, o_i, o_og = 0, HA_K, 2 * HA_K, 2 * HA_K + HA_V
    o_qb = o_og + HA_V
    o_kb = o_qb + RB_K
    o_vb = o_kb + RB_K
    o_ogb = o_vb + RB_V
    o_ma = o_ogb + RB_V
    o_mb = o_ma + x.shape[1]

    f = lb + (1.0 - lb) * _sigmoid(z_scr[:, o_f:o_f + HA_K])
    k_scr[...] = 1.0 - f
    g = jnp.log(f)
    g_hi = g.astype(BF16)
    g_lo = (g - g_hi.astype(F32)).astype(BF16)
    tri = tri_scr[...]
    cum_scr[...] = (jnp.dot(tri, g_hi, preferred_element_type=F32)
                    + jnp.dot(tri, g_lo, preferred_element_type=F32))

    yield

    lvl = lvl_scr[...]
    base_level = base_block.bit_length() - 1
    peak = jnp.zeros((1, HA_DK), F32)
    for h in range(HA_HEADS):
        cols = slice(h * HA_DK, (h + 1) * HA_DK)
        q = _silu(z_scr[:, o_q + h * HA_DK:o_q + (h + 1) * HA_DK])
        k = k_scr[:, cols]
        v = z_scr[:, o_i + h * HA_DV:o_i + (h + 1) * HA_DV].astype(BF16)
        cumh = cum_scr[:, cols]
        scores = jnp.zeros((t, t), F32)
        for level in range(base_level, n_levels + 1):
            if level > base_level:
                ref = _level_ref(cum_scr, cumh, cols, 1 << level)
                e = jnp.exp(-jnp.abs(cumh - ref))
                qs, ks = q * e, k * e
                mask = lvl == level
            elif base_block == 1:
                qs, ks = q, k
                mask = lvl == 0
            else:
                shift = cumh - _level_ref(cum_scr, cumh, cols, base_block)
                qs, ks = q * jnp.exp(shift), k * jnp.exp(-shift)
                big = jnp.maximum(jnp.abs(qs), jnp.abs(ks))
                peak = jnp.maximum(peak, jnp.max(big, axis=0, keepdims=True))
                mask = (lvl >= 0) & (lvl <= base_level)
            scores = jnp.where(mask, _bdot(qs, ks, _NT), scores)
        st = jnp.where(first, 0.0, sa_scr[h])
        last = cum_scr[t - 1:t, cols]
        o = _bdot(scores, v) + _bdot(q * jnp.exp(cumh), st, _NT)
        sa_scr[h] = st * jnp.exp(last) + _bdot(v, k * jnp.exp(last - cumh), _TN)
        on = o * lax.rsqrt(jnp.mean(o * o, axis=-1, keepdims=True) + EPS) * ghg_ref[:, cols]
        og = z_scr[:, o_og + h * HA_DV:o_og + (h + 1) * HA_DV]
        oa_scr[:, cols] = (on * _silu(og)).astype(BF16)
        yield

    flag_ref[0] = jnp.broadcast_to(peak, (SUBLANES, HA_DK))

    cosf = cos_ref[...]
    sins = sin_ref[...]
    pos = lax.broadcasted_iota(jnp.int32, (t, RB_DK), 0).astype(F32)
    for h in range(RB_HEADS):
        log_gamma = math.log(1.0 - 2.0 ** (-5.0 - h))
        qb = z_scr[:, o_qb + h * RB_DK:o_qb + (h + 1) * RB_DK]
        kb = z_scr[:, o_kb + h * RB_DK:o_kb + (h + 1) * RB_DK]
        qr = (qb * cosf + pltpu.roll(qb, RB_DK // 2, 1) * sins) * (RB_DK ** -0.5)
        kr = kb * cosf + pltpu.roll(kb, RB_DK // 2, 1) * sins
        vcols = slice(h * RB_DV, (h + 1) * RB_DV)
        vb = z_scr[:, o_vb + h * RB_DV:o_vb + (h + 1) * RB_DV].astype(BF16)
        s = _bdot(qr, kr, _NT) * retw_scr[h]
        sb = jnp.where(first, 0.0, sb_scr[h])
        q_dec = jnp.exp(log_gamma * (pos + 1.0))
        k_dec = jnp.exp(log_gamma * (float(t - 1) - pos))
        o = _bdot(s, vb) + _bdot(qr * q_dec, sb)
        sb_scr[h] = math.exp(log_gamma * t) * sb + _bdot(kr * k_dec, vb, _TN)
        mu = jnp.mean(o, axis=-1, keepdims=True)
        oc = o - mu
        var = jnp.mean(oc * oc, axis=-1, keepdims=True)
        on = oc * lax.rsqrt(var + EPS) * gret_ref[:, vcols]
        og = z_scr[:, o_ogb + h * RB_DV:o_ogb + (h + 1) * RB_DV]
        ob_scr[:, vcols] = (on * _silu(og)).astype(BF16)
        yield

    d = x.shape[1]
    ya = jnp.dot(oa_scr[...], wa_ref[...], preferred_element_type=F32)
    yb = jnp.dot(ob_scr[...], wb_ref[...], preferred_element_type=F32)
    y = _sigmoid(z_scr[:, o_ma:o_ma + d]) * ya + _sigmoid(z_scr[:, o_mb:o_mb + d]) * yb
    o_ref[0] = x + gt1 * _bdot(y, wo_ref[...])


def _mixer_kernel(xp_ref, xm_ref, modp_ref, modm_ref, g1_ref, win_ref, lbl_ref, ghg_ref,
                  gret_ref, wa_ref, wb_ref, wo_ref, cos_ref, sin_ref, o_ref, flag_ref,
                  z0_scr, z1_scr, sa_scr, sb_scr, cum_scr, k_scr, retw_scr, lvl_scr, tri_scr,
                  oa_scr, ob_scr, *, layer, depth, tiles_per_seq, base_block):
    t = xp_ref.shape[1]
    n_levels = t.bit_length() - 1
    g = pl.program_id(0)

    @pl.when(g == 0)
    def _build_constants():
        row = lax.broadcasted_iota(jnp.int32, (t, t), 0)
        col = lax.broadcasted_iota(jnp.int32, (t, t), 1)
        diff = row ^ col
        lvl = jnp.zeros((t, t), jnp.int32)
        for bit in range(n_levels):
            lvl = lvl + (diff >= (1 << bit)).astype(jnp.int32)
        lvl_scr[...] = jnp.where(col > row, -1, lvl)
        tri_scr[...] = (col <= row).astype(BF16)
        dist = jnp.abs(row - col).astype(F32)
        visible = (col // CHUNK) <= (row // CHUNK)
        for h in range(RB_HEADS):
            log_gamma = math.log(1.0 - 2.0 ** (-5.0 - h))
            retw_scr[h] = jnp.where(visible, jnp.exp(log_gamma * dist), 0.0)
        z1_scr[...] = jnp.zeros_like(z1_scr)
        sa_scr[...] = jnp.zeros_like(sa_scr)
        sb_scr[...] = jnp.zeros_like(sb_scr)

    first = lax.rem(jnp.maximum(g - 1, 0), tiles_per_seq) == 0

    def step(z_write, z_read):
        proj = _project(xp_ref, modp_ref, g1_ref, win_ref, z_write)
        mix = _mix(z_read, xm_ref, modm_ref, first, lbl_ref, ghg_ref, gret_ref, wa_ref, wb_ref,
                   wo_ref, cos_ref, sin_ref, o_ref, flag_ref, sa_scr, sb_scr, cum_scr, k_scr,
                   retw_scr, lvl_scr, tri_scr, oa_scr, ob_scr, layer=layer, depth=depth,
                   base_block=base_block)
        live = [proj, mix]
        while live:
            for gen in list(live):
                if next(gen, StopIteration) is StopIteration:
                    live.remove(gen)

    @pl.when(lax.rem(g, 2) == 0)
    def _even():
        step(z0_scr, z1_scr)

    @pl.when(lax.rem(g, 2) == 1)
    def _odd():
        step(z1_scr, z0_scr)


def _const_spec(shape):
    return pl.BlockSpec(shape, lambda g: (0,) * len(shape), pipeline_mode=pl.Buffered(1))


def _mixer(x, mod, g1, w_in, lb_logits, g_hgrn, g_ret, w_a, w_b, w_o, cos_t, sin_t, *, layer,
           base_block):
    bsz, seq, d = x.shape
    depth = mod.shape[0]
    t = MIX_TILE
    n_in = w_in.shape[1]
    tps = seq // t
    n_tiles = bsz * tps
    kern = functools.partial(_mixer_kernel, layer=layer, depth=depth, tiles_per_seq=tps,
                             base_block=base_block)

    def proj_tile(g):
        return jnp.minimum(g, n_tiles - 1)

    def mix_tile(g):
        return jnp.maximum(g - 1, 0)

    return pl.pallas_call(
        kern,
        grid=(n_tiles + 1,),
        in_specs=[
            pl.BlockSpec((1, t, d), lambda g: (proj_tile(g) // tps, proj_tile(g) % tps, 0)),
            pl.BlockSpec((1, t, d), lambda g: (mix_tile(g) // tps, mix_tile(g) % tps, 0)),
            pl.BlockSpec((1, 1, 6, d), lambda g: (layer, proj_tile(g) // tps, 0, 0)),
            pl.BlockSpec((1, 1, 6, d), lambda g: (layer, mix_tile(g) // tps, 0, 0)),
            _const_spec((1, d)),
            _const_spec((d, n_in)),
            _const_spec((depth, HA_K)),
            _const_spec((1, HA_V)),
            _const_spec((1, RB_V)),
            _const_spec((HA_V, d)),
            _const_spec((RB_V, d)),
            _const_spec((d, d)),
            pl.BlockSpec((t, RB_DK), lambda g: (mix_tile(g) % tps, 0)),
            pl.BlockSpec((t, RB_DK), lambda g: (mix_tile(g) % tps, 0)),
        ],
        out_specs=[
            pl.BlockSpec((1, t, d), lambda g: (mix_tile(g) // tps, mix_tile(g) % tps, 0)),
            pl.BlockSpec((1, SUBLANES, HA_DK), lambda g: (g, 0, 0)),
        ],
        out_shape=[
            jax.ShapeDtypeStruct((bsz, seq, d), F32),
            jax.ShapeDtypeStruct((n_tiles + 1, SUBLANES, HA_DK), F32),
        ],
        scratch_shapes=[
            pltpu.VMEM((t, n_in), F32),
            pltpu.VMEM((t, n_in), F32),
            pltpu.VMEM((HA_HEADS, HA_DV, HA_DK), F32),
            pltpu.VMEM((RB_HEADS, RB_DK, RB_DV), F32),
            pltpu.VMEM((t, HA_K), F32),
            pltpu.VMEM((t, HA_K), F32),
            pltpu.VMEM((RB_HEADS, t, t), F32),
            pltpu.VMEM((t, t), jnp.int32),
            pltpu.VMEM((t, t), BF16),
            pltpu.VMEM((t, HA_V), BF16),
            pltpu.VMEM((t, RB_V), BF16),
        ],
        compiler_params=pltpu.CompilerParams(
            dimension_semantics=("arbitrary",), vmem_limit_bytes=VMEM_LIMIT_BYTES),
        name=f"mixer_layer{layer}_block{base_block}",
    )(x, x, mod, mod, g1, w_in, lb_logits, g_hgrn, g_ret, w_a, w_b, w_o, cos_t, sin_t)


def _mixer_layer(x, *args, layer):
    fast, peak = _mixer(x, *args, layer=layer, base_block=FAST_BLOCK)
    safe = jnp.max(peak) <= SAFE_PEAK
    return lax.cond(safe, lambda: fast,
                    lambda: _mixer(x, *args, layer=layer, base_block=1)[0])


def _expert_shift(a, delta):
    return pltpu.roll(a, (-delta) % a.shape[0], 0)


def _route(scores, biased):
    idx = lax.broadcasted_iota(jnp.int32, scores.shape, 0)
    pos = idx & (EXP_PER_GROUP - 1)
    grp = idx // EXP_PER_GROUP
    rank = jnp.zeros(scores.shape, jnp.int32)
    for delta in range(-(EXP_PER_GROUP - 1), EXP_PER_GROUP):
        if delta == 0:
            continue
        other = _expert_shift(biased, delta)
        ahead = (other > biased) | ((other == biased) & (delta < 0))
        in_grp = (pos + delta >= 0) & (pos + delta < EXP_PER_GROUP)
        rank = rank + (ahead & in_grp).astype(jnp.int32)
    top = rank < TOP_K
    kept = jnp.where(top, biased, 0.0)
    grp_score = kept
    for delta in range(-(EXP_PER_GROUP - 1), EXP_PER_GROUP):
        if delta == 0:
            continue
        in_grp = (pos + delta >= 0) & (pos + delta < EXP_PER_GROUP)
        grp_score = grp_score + jnp.where(in_grp, _expert_shift(kept, delta), 0.0)
    grank = jnp.zeros(scores.shape, jnp.int32)
    for dg in range(-(N_GROUPS - 1), N_GROUPS):
        if dg == 0:
            continue
        other = _expert_shift(grp_score, dg * EXP_PER_GROUP)
        ahead = (other > grp_score) | ((other == grp_score) & (dg < 0))
        in_rng = (grp + dg >= 0) & (grp + dg < N_GROUPS)
        grank = grank + (ahead & in_rng).astype(jnp.int32)
    sel = top & (grank == 0)
    w = jnp.where(sel, scores, 0.0)
    return w / jnp.sum(w, axis=0, keepdims=True), sel


def _router_kernel(x_ref, mod_ref, g2_ref, wr_ref, bcol_ref, p_ref, idx_ref, cnt_ref,
                   before_scr, run_scr):
    tm, d = x_ref.shape
    half = d // 2
    i = pl.program_id(0)

    @pl.when(i == 0)
    def _init():
        row = lax.broadcasted_iota(jnp.int32, (LANES, LANES), 0)
        col = lax.broadcasted_iota(jnp.int32, (LANES, LANES), 1)
        before_scr[...] = (row < col).astype(BF16)
        run_scr[...] = jnp.zeros_like(run_scr)

    x = x_ref[...]
    modv = mod_ref[0, 0]
    sh2, sc2 = modv[3:4], modv[4:5]
    ms = jnp.mean(x * x, axis=-1, keepdims=True)
    h = (x * lax.rsqrt(ms + EPS)) * g2_ref[...] * (1.0 + sc2) + sh2
    h_hi = h.astype(BF16)
    h_hi32 = h_hi.astype(F32)
    h_lo = (h - h_hi32).astype(BF16)
    wr = wr_ref[...]
    w_hi = wr.astype(BF16)
    w_lo = (wr - w_hi.astype(F32)).astype(BF16)
    logits = (jnp.dot(h_hi, w_hi, preferred_element_type=F32)
              + jnp.dot(h_lo, w_hi, preferred_element_type=F32)
              + jnp.dot(h_hi, w_lo, preferred_element_type=F32))
    scores = _sigmoid(logits.T[0:N_EXPERTS, :])
    comb, sel = _route(scores, scores + bcol_ref[...])

    bits = pltpu.bitcast(h_hi32, jnp.uint32)
    p_ref[:, 0:half] = (bits[:, 0:half] >> 16) | bits[:, half:d]
    comb_rows = jnp.concatenate([comb, jnp.zeros((LANES - N_EXPERTS, tm), F32)], axis=0).T
    p_ref[:, half:half + LANES] = pltpu.bitcast(comb_rows, jnp.uint32)

    sel_f = sel.astype(F32)
    rows = [sel_f[e:e + 1, :] for e in range(N_EXPERTS)]
    bucket = jnp.zeros((1, tm), F32)
    for g in range(N_GROUPS):
        grp_rows = rows[g * EXP_PER_GROUP:(g + 1) * EXP_PER_GROUP]
        bucket = bucket + float(g * N_PAIRS) * functools.reduce(jnp.maximum, grp_rows)
        for p, (a, b) in enumerate(PAIRS):
            if p:
                bucket = bucket + float(p) * (grp_rows[a] * grp_rows[b])
    bucket = bucket.astype(jnp.int32)
    onehot = (lax.broadcasted_iota(jnp.int32, (BUCKET_ROWS, tm), 0) == bucket).astype(F32)
    chunks = [onehot[:, c * LANES:(c + 1) * LANES] for c in range(tm // LANES)]
    inside = jnp.dot(jnp.concatenate(chunks, axis=0).astype(BF16), before_scr[...],
                     preferred_element_type=F32)
    run = run_scr[...]
    seen = run[:, 0:1]
    ranks = []
    for c, chunk in enumerate(chunks):
        earlier = inside[c * BUCKET_ROWS:(c + 1) * BUCKET_ROWS, :] + seen
        ranks.append(jnp.sum(chunk * earlier, axis=0, keepdims=True))
        seen = seen + jnp.sum(chunk, axis=1, keepdims=True)
    rank = jnp.concatenate(ranks, axis=1)
    run = jnp.broadcast_to(seen, run.shape)
    run_scr[...] = run
    idx_ref[0] = jnp.concatenate(
        [bucket, rank.astype(jnp.int32), jnp.zeros((SUBLANES - 2, tm), jnp.int32)], axis=0)
    cnt_ref[...] = run.astype(jnp.int32)


def _router(x2d, mod, g2, wr_pad, b_col, *, layer, seq):
    n_tok, d = x2d.shape
    tm = ROUTE_TILE
    n_tiles = n_tok // tm
    row_w = d // 2 + LANES
    return pl.pallas_call(
        _router_kernel,
        grid=(n_tiles,),
        in_specs=[
            pl.BlockSpec((tm, d), lambda i: (i, 0)),
            pl.BlockSpec((1, 1, 6, d), lambda i: (layer, (i * tm) // seq, 0, 0)),
            pl.BlockSpec((1, d), lambda i: (0, 0)),
            pl.BlockSpec((d, LANES), lambda i: (0, 0)),
            pl.BlockSpec((N_EXPERTS, 1), lambda i: (0, 0)),
        ],
        out_specs=[
            pl.BlockSpec((tm, row_w), lambda i: (i, 0)),
            pl.BlockSpec((1, SUBLANES, tm), lambda i: (i, 0, 0)),
            pl.BlockSpec((BUCKET_ROWS, LANES), lambda i: (0, 0)),
        ],
        out_shape=[
            jax.ShapeDtypeStruct((n_tok, row_w), jnp.uint32),
            jax.ShapeDtypeStruct((n_tiles, SUBLANES, tm), jnp.int32),
            jax.ShapeDtypeStruct((BUCKET_ROWS, LANES), jnp.int32),
        ],
        scratch_shapes=[
            pltpu.VMEM((LANES, LANES), BF16),
            pltpu.VMEM((BUCKET_ROWS, LANES), F32),
        ],
        compiler_params=pltpu.CompilerParams(
            dimension_semantics=("arbitrary",), vmem_limit_bytes=VMEM_LIMIT_BYTES),
        name=f"moe_router{layer}",
    )(x2d, mod, g2, wr_pad, b_col)


def _dispatch_kernel(dest_ref, p_ref, init_ref, hs_ref, sem):
    del init_ref
    rows = p_ref.shape[0] * SUBLANES
    for r in range(rows):
        pltpu.make_async_copy(p_ref.at[r // SUBLANES, pl.ds(r % SUBLANES, 1), :],
                              hs_ref.at[pl.ds(dest_ref[0, 0, r], 1), :],
                              sem).start(priority=r % DMA_PRIORITIES)
    for _ in range(rows):
        pltpu.make_async_copy(p_ref.at[0, pl.ds(0, 1), :], hs_ref.at[pl.ds(0, 1), :],
                              sem).wait()


def _dispatch(p, dest2d, init, *, layer):
    n_tok, row_w = p.shape
    td = dest2d.shape[2]
    p = p.reshape(n_tok // SUBLANES, SUBLANES, row_w)
    return pl.pallas_call(
        _dispatch_kernel,
        grid=(n_tok // td,),
        in_specs=[
            pl.BlockSpec((1, 1, td), lambda i: (i, 0, 0), memory_space=pltpu.SMEM),
            pl.BlockSpec((td // SUBLANES, SUBLANES, row_w), lambda i: (i, 0, 0)),
            pl.BlockSpec(memory_space=pl.ANY),
        ],
        out_specs=pl.BlockSpec(memory_space=pl.ANY),
        out_shape=jax.ShapeDtypeStruct(init.shape, jnp.uint32),
        scratch_shapes=[pltpu.SemaphoreType.DMA(())],
        input_output_aliases={2: 0},
        compiler_params=pltpu.CompilerParams(
            dimension_semantics=("arbitrary",), vmem_limit_bytes=VMEM_LIMIT_BYTES),
        name=f"moe_dispatch{layer}",
    )(dest2d, p, init)


def _expert_kernel(e1_ref, e2_ref, valid_ref, hs_ref, wg1_ref, wu1_ref, wd1_ref,
                   wg2_ref, wu2_ref, wd2_ref, y_ref, wg_scr, wu_scr, wd_scr):
    j = pl.program_id(0)
    half = hs_ref.shape[1] - LANES
    prev = jnp.maximum(j - 1, 0)
    new_pair = (j == 0) | (e1_ref[j] != e1_ref[prev]) | (e2_ref[j] != e2_ref[prev])

    @pl.when((valid_ref[j] != 0) & new_pair)
    def _cast_weights():
        for slot, (wg_ref, wu_ref, wd_ref) in enumerate(((wg1_ref, wu1_ref, wd1_ref),
                                                         (wg2_ref, wu2_ref, wd2_ref))):
            wg_scr[slot] = wg_ref[0, 0].astype(BF16)
            wu_scr[slot] = wu_ref[0, 0].astype(BF16)
            wd_scr[slot] = wd_ref[0, 0].astype(BF16)

    @pl.when(valid_ref[j] != 0)
    def _compute():
        packed = hs_ref[:, 0:half]
        lo = pltpu.bitcast(packed << 16, F32)
        hi = pltpu.bitcast(packed & jnp.uint32(0xFFFF0000), F32)
        h = jnp.concatenate([lo, hi], axis=1).astype(BF16)
        comb = pltpu.bitcast(hs_ref[:, half:half + LANES], F32)
        lane = lax.broadcasted_iota(jnp.int32, comb.shape, 1)

        def expert(e, slot):
            ce = jnp.sum(jnp.where(lane == e, comb, 0.0), axis=-1, keepdims=True)
            hg = jnp.dot(h, wg_scr[slot], preferred_element_type=F32)
            hu = jnp.dot(h, wu_scr[slot], preferred_element_type=F32)
            return ce * _bdot(_silu(hg) * hu, wd_scr[slot])

        y_ref[...] = expert(e1_ref[j], 0) + expert(e2_ref[j], 1)

    @pl.when(valid_ref[j] == 0)
    def _empty():
        y_ref[...] = jnp.zeros_like(y_ref)


def _experts(hs, e1, e2, valid, w_gate, w_up, w_down, *, layer):
    n_sorted, row_w = hs.shape
    _, _, d, d_exp = w_gate.shape
    ts = SORT_TILE
    first = lambda j, e1, e2, valid: (layer, e1[j], 0, 0)
    second = lambda j, e1, e2, valid: (layer, e2[j], 0, 0)
    grid_spec = pltpu.PrefetchScalarGridSpec(
        num_scalar_prefetch=3,
        grid=(n_sorted // ts,),
        in_specs=[
            pl.BlockSpec((ts, row_w), lambda j, e1, e2, valid: (j, 0)),
            pl.BlockSpec((1, 1, d, d_exp), first),
            pl.BlockSpec((1, 1, d, d_exp), first),
            pl.BlockSpec((1, 1, d_exp, d), first),
            pl.BlockSpec((1, 1, d, d_exp), second),
            pl.BlockSpec((1, 1, d, d_exp), second),
            pl.BlockSpec((1, 1, d_exp, d), second),
        ],
        out_specs=pl.BlockSpec((ts, d), lambda j, e1, e2, valid: (j, 0)),
        scratch_shapes=[
            pltpu.VMEM((TOP_K, d, d_exp), BF16),
            pltpu.VMEM((TOP_K, d, d_exp), BF16),
            pltpu.VMEM((TOP_K, d_exp, d), BF16),
        ],
    )
    return pl.pallas_call(
        _expert_kernel,
        grid_spec=grid_spec,
        out_shape=jax.ShapeDtypeStruct((n_sorted, d), F32),
        compiler_params=pltpu.CompilerParams(
            dimension_semantics=("arbitrary",), vmem_limit_bytes=VMEM_LIMIT_BYTES),
        name=f"moe_experts{layer}",
    )(e1, e2, valid, hs, w_gate, w_up, w_down, w_gate, w_up, w_down)


def _combine_kernel(dest_ref, x_ref, mod_ref, gf_ref, ys_ref, o_ref, y_scr, sem, *, final_norm):
    tc, d = x_ref.shape
    for r in range(tc):
        pltpu.make_async_copy(ys_ref.at[pl.ds(dest_ref[0, 0, r], 1), :],
                              y_scr.at[r // SUBLANES, pl.ds(r % SUBLANES, 1), :],
                              sem).start(priority=r % DMA_PRIORITIES)
    for _ in range(tc):
        pltpu.make_async_copy(ys_ref.at[pl.ds(0, 1), :], y_scr.at[0, pl.ds(0, 1), :],
                              sem).wait()

    gt2 = mod_ref[0, 0][5:6]
    out = x_ref[...] + gt2 * y_scr[...].reshape(tc, d)
    if final_norm:
        ms = jnp.mean(out * out, axis=-1, keepdims=True)
        out = out * lax.rsqrt(ms + EPS) * gf_ref[...]
    o_ref[...] = out


def _combine(x2d, mod, g_final, ys, dest2d, *, layer, seq, final_norm):
    n_tok, d = x2d.shape
    tc = dest2d.shape[2]
    kern = functools.partial(_combine_kernel, final_norm=final_norm)
    return pl.pallas_call(
        kern,
        grid=(n_tok // tc,),
        in_specs=[
            pl.BlockSpec((1, 1, tc), lambda i: (i, 0, 0), memory_space=pltpu.SMEM),
            pl.BlockSpec((tc, d), lambda i: (i, 0)),
            pl.BlockSpec((1, 1, 6, d), lambda i: (layer, (i * tc) // seq, 0, 0)),
            pl.BlockSpec((1, d), lambda i: (0, 0)),
            pl.BlockSpec(memory_space=pl.ANY),
        ],
        out_specs=pl.BlockSpec((tc, d), lambda i: (i, 0)),
        out_shape=jax.ShapeDtypeStruct((n_tok, d), F32),
        scratch_shapes=[pltpu.VMEM((tc // SUBLANES, SUBLANES, d), F32),
                        pltpu.SemaphoreType.DMA(())],
        compiler_params=pltpu.CompilerParams(
            dimension_semantics=("arbitrary",), vmem_limit_bytes=VMEM_LIMIT_BYTES),
        name=f"moe_combine{layer}",
    )(dest2d, x2d, mod, g_final, ys)


def _moe(x2d, mod, g2, wr_pad, b_col, w_gate, w_up, w_down, g_final, sorted_buf, *, layer, seq,
         final_norm):
    n_tok, _ = x2d.shape
    p, idx, cnt = _router(x2d, mod, g2, wr_pad, b_col, layer=layer, seq=seq)

    n_buckets = N_GROUPS * N_PAIRS
    counts = cnt[:n_buckets, 0]
    padded = ((counts + SORT_TILE - 1) // SORT_TILE) * SORT_TILE
    ends = jnp.cumsum(padded)
    starts = ends - padded
    dest = starts[idx[:, 0, :].reshape(n_tok)] + idx[:, 1, :].reshape(n_tok)
    n_tiles = sorted_buf.shape[0] // SORT_TILE
    tile_start = jnp.arange(n_tiles, dtype=jnp.int32) * SORT_TILE
    tile_bucket = jnp.sum((ends[None, :] <= tile_start[:, None]).astype(jnp.int32), axis=1)
    valid = (tile_bucket < n_buckets).astype(jnp.int32)
    tb = jnp.minimum(tile_bucket, n_buckets - 1)
    grp, pair = tb // N_PAIRS, tb % N_PAIRS
    pair_lo = jnp.asarray([a for a, _ in PAIRS], jnp.int32)
    pair_hi = jnp.asarray([b for _, b in PAIRS], jnp.int32)
    e1 = grp * EXP_PER_GROUP + pair_lo[pair]
    e2 = grp * EXP_PER_GROUP + pair_hi[pair]

    hs = _dispatch(p, dest.reshape(n_tok // DISPATCH_TILE, 1, DISPATCH_TILE), sorted_buf,
                   layer=layer)
    ys = _experts(hs, e1, e2, valid, w_gate, w_up, w_down, layer=layer)
    out = _combine(x2d, mod, g_final, ys, dest.reshape(n_tok // COMBINE_TILE, 1, COMBINE_TILE),
                   layer=layer, seq=seq, final_norm=final_norm)
    return out, hs


def _rope_tables(seq):
    half = RB_DK // 2
    inv = ROPE_BASE ** (-jnp.arange(half, dtype=F32) / half)
    ang = jnp.arange(seq, dtype=F32)[:, None] * inv[None, :]
    cos, sin = jnp.cos(ang), jnp.sin(ang)
    return jnp.concatenate([cos, cos], axis=-1), jnp.concatenate([-sin, sin], axis=-1)


def kernel(x, c, w_ada, b_ada, g_norm1, g_norm2, w_in, lb_logits, g_hgrn, g_ret, w_branch_a,
           w_branch_b, w_out, w_router, b_router, w_exp_gate, w_exp_up, w_exp_down, g_final):
    bsz, seq, d = x.shape
    depth = w_ada.shape[0]
    n_tok = bsz * seq
    assert seq % MIX_TILE == 0 and MIX_TILE % CHUNK == 0
    assert seq % ROUTE_TILE == 0 and seq % COMBINE_TILE == 0
    assert n_tok % DISPATCH_TILE == 0 and n_tok % SORT_TILE == 0
    assert w_ada.shape[2] == 6 * d and w_ada.shape[2] % MOD_COLS == 0
    assert N_GROUPS * N_PAIRS <= BUCKET_ROWS and TOP_K == 2

    mod = _modulation(c, w_ada, b_ada).reshape(depth, bsz, 6, d)
    cos_t, sin_t = _rope_tables(seq)
    wr_pad = jnp.pad(w_router, ((0, 0), (0, LANES - N_EXPERTS)))
    b_col = b_router.astype(F32).reshape(N_EXPERTS, 1)
    lbl = lb_logits.astype(F32)
    sorted_rows = n_tok + N_GROUPS * N_PAIRS * SORT_TILE
    sorted_buf = jnp.zeros((sorted_rows, d // 2 + LANES), jnp.uint32)

    for l in range(depth):
        x = _mixer_layer(x, mod, g_norm1[l].reshape(1, d), w_in[l].astype(BF16), lbl,
                         g_hgrn[l].reshape(1, HA_V), g_ret[l].reshape(1, RB_V),
                         w_branch_a[l].astype(BF16), w_branch_b[l].astype(BF16),
                         w_out[l].astype(BF16), cos_t, sin_t, layer=l)
        x, sorted_buf = _moe(x.reshape(n_tok, d), mod, g_norm2[l].reshape(1, d), wr_pad, b_col,
                             w_exp_gate, w_exp_up, w_exp_down, g_final.reshape(1, d), sorted_buf,
                             layer=l, seq=seq, final_norm=(l == depth - 1))
        x = x.reshape(bsz, seq, d)
    return x
```
